```python
import math
import jax, jax.numpy as jnp
from jax import lax
import numpy as np

D_MODEL = 1024
BATCH = 8
SEQ = 2048
DEPTH = 2
DEC_BATCH = 128
DEC_SEQ = 4
PAST_LEN = 16384
PAGE_SIZE = 128

N_RET_HEADS = 4
HD_QK = 128
HD_V = 256
D_RET_QK = N_RET_HEADS * HD_QK
D_RET_V = N_RET_HEADS * HD_V
D_CONV = 1024
CONV_W = 3
D_FF = 2816
CHUNK = 128
ROPE_BASE = 10000.0
NORM_EPS = 1e-6
N_NORMS = 6
IN_SIZES = (D_RET_QK, D_RET_QK, D_RET_V, D_RET_V, D_CONV, D_CONV, D_CONV, 2 * D_MODEL)
N_IN = sum(IN_SIZES)

kernel_name = "retention_shortconv_gated_hybrid_step"


def rmsnorm(x, g):
    xf = x.astype(jnp.float32)
    y = xf * lax.rsqrt(jnp.mean(xf * xf, axis=-1, keepdims=True) + NORM_EPS)
    return (y * g.astype(jnp.float32)).astype(x.dtype)


def head_rmsnorm(x):
    xf = x.astype(jnp.float32)
    return (xf * lax.rsqrt(jnp.mean(xf * xf, axis=-1, keepdims=True) + NORM_EPS)).astype(x.dtype)


def swiglu(x, w_up, w_down):
    gate, up = jnp.split(x @ w_up, 2, axis=-1)
    return (jax.nn.silu(gate) * up) @ w_down


def rotary(x, pos):
    d = x.shape[-1]
    half = d // 2
    inv_freq = ROPE_BASE ** (-(jnp.arange(half, dtype=jnp.float32) * 2.0 / d))
    ang = pos.astype(jnp.float32)[:, None] * inv_freq[None, :]
    cos = jnp.cos(ang)[None, :, None, :]
    sin = jnp.sin(ang)[None, :, None, :]
    xf = x.astype(jnp.float32)
    x1, x2 = xf[..., :half], xf[..., half:]
    return jnp.concatenate([x1 * cos - x2 * sin, x2 * cos + x1 * sin], axis=-1).astype(x.dtype)


def log_gammas():
    gam = 1.0 - jnp.exp(jnp.linspace(math.log(1.0 / 32), math.log(1.0 / 512), N_RET_HEADS, dtype=jnp.float32))
    return jnp.log(gam)


def retention(q, k, v, s0):
    B, T, H, _ = q.shape
    C = math.gcd(T, CHUNK)
    n = T // C
    lg = log_gammas()
    idx = jnp.arange(C, dtype=jnp.float32)
    diff = idx[:, None] - idx[None, :]
    dmat = jnp.where(diff[None] >= 0, jnp.exp(jnp.maximum(diff, 0.0)[None] * lg[:, None, None]), 0.0)
    xi = jnp.exp((idx[:, None] + 1.0) * lg[None, :])[None, :, :, None]
    zeta = jnp.exp((C - 1.0 - idx)[:, None] * lg[None, :])[None, :, :, None]
    g_chunk = jnp.exp(C * lg)[None, :, None, None]

    def to_chunks(t):
        return t.astype(jnp.float32).reshape(B, n, C, H, t.shape[-1]).transpose(1, 0, 2, 3, 4)

    def step(s, blk):
        qc, kc, vc = blk
        scores = jnp.einsum('bihd,bjhd->bhij', qc, kc) * dmat[None]
        inner = jnp.einsum('bhij,bjhe->bihe', scores, vc)
        cross = jnp.einsum('bihd,bhde->bihe', qc, s) * xi
        s_new = g_chunk * s + jnp.einsum('bjhd,bjhe->bhde', kc * zeta, vc)
        return s_new, inner + cross

    s_fin, o = lax.scan(step, s0.astype(jnp.float32), (to_chunks(q), to_chunks(k), to_chunks(v)))
    o = o.transpose(1, 0, 2, 3, 4).reshape(B, T, H, HD_V)
    return o.astype(v.dtype), s_fin.astype(s0.dtype)


def short_conv(a, buf, w):
    T = a.shape[1]
    full = jnp.concatenate([buf.astype(a.dtype), a], axis=1)
    z = sum(w[i] * full[:, i:i + T] for i in range(CONV_W))
    return z, full[:, -(CONV_W - 1):]


def mixer(u, pos, s0, buf0, w_in, conv_w, w_ret_out, w_conv_out, w_o):
    B, T, _ = u.shape
    proj = u @ w_in
    cuts = [int(c) for c in np.cumsum(IN_SIZES)[:-1]]
    q, k, v, g, bg, cg, xc, gates = jnp.split(proj, cuts, axis=-1)
    q = rotary(q.reshape(B, T, N_RET_HEADS, HD_QK), pos)
    k = rotary(k.reshape(B, T, N_RET_HEADS, HD_QK), pos) * (HD_QK ** -0.5)
    v = v.reshape(B, T, N_RET_HEADS, HD_V)
    o, s_new = retention(q, k, v, s0)
    o = head_rmsnorm(o).reshape(B, T, D_RET_V)
    o_ret = (jax.nn.silu(g) * o) @ w_ret_out
    z, buf_new = short_conv(cg * xc, buf0, conv_w)
    o_conv = (bg * z) @ w_conv_out
    gate_r, gate_c = jnp.split(jax.nn.sigmoid(gates), 2, axis=-1)
    merged = gate_r * o_ret + gate_c * o_conv
    return merged @ w_o, s_new, buf_new


def layer(x, pos, s0, buf0, norms, w_ffn1_up, w_ffn1_down, w_in, conv_w, w_ret_out, w_conv_out, w_o,
          w_ffn2_up, w_ffn2_down):
    h = x + 0.5 * rmsnorm(swiglu(rmsnorm(x, norms[0]), w_ffn1_up, w_ffn1_down), norms[1])
    m, s_new, buf_new = mixer(rmsnorm(h, norms[2]), pos, s0, buf0, w_in, conv_w, w_ret_out, w_conv_out, w_o)
    h = h + rmsnorm(m, norms[3])
    h = h + 0.5 * rmsnorm(swiglu(rmsnorm(h, norms[4]), w_ffn2_up, w_ffn2_down), norms[5])
    return h, s_new, buf_new


def setup_inputs(seed: int = 0) -> dict:
    key = jax.random.key(seed)
    ks = jax.random.split(key, 16)
    f32 = jnp.float32

    def nrm(k, shape, scale):
        return jax.random.normal(k, shape, f32) * scale

    return {
        "x_prompt": nrm(ks[0], (BATCH, SEQ, D_MODEL), 1.0),
        "x_sample": nrm(ks[1], (DEC_BATCH, DEC_SEQ, D_MODEL), 1.0),
        "state_ret": nrm(ks[2], (DEPTH, DEC_BATCH, N_RET_HEADS, HD_QK, HD_V), 1.0),
        "state_conv": nrm(ks[3], (DEPTH, DEC_BATCH, CONV_W - 1, D_CONV), 0.5),
        "norms": 1.0 + nrm(ks[4], (DEPTH, N_NORMS, D_MODEL), 0.05),
        "w_ffn1_up": nrm(ks[5], (DEPTH, D_MODEL, 2 * D_FF), D_MODEL ** -0.5),
        "w_ffn1_down": nrm(ks[6], (DEPTH, D_FF, D_MODEL), D_FF ** -0.5),
        "w_in": nrm(ks[7], (DEPTH, D_MODEL, N_IN), D_MODEL ** -0.5),
        "conv_w": nrm(ks[8], (DEPTH, CONV_W, D_CONV), CONV_W ** -0.5),
        "w_ret_out": nrm(ks[9], (DEPTH, D_RET_V, D_MODEL), D_RET_V ** -0.5),
        "w_conv_out": nrm(ks[10], (DEPTH, D_CONV, D_MODEL), D_CONV ** -0.5),
        "w_o": nrm(ks[11], (DEPTH, D_MODEL, D_MODEL), D_MODEL ** -0.5),
        "w_ffn2_up": nrm(ks[12], (DEPTH, D_MODEL, 2 * D_FF), D_MODEL ** -0.5),
        "w_ffn2_down": nrm(ks[13], (DEPTH, D_FF, D_MODEL), D_FF ** -0.5),
    }


def reference(x_prompt, x_sample, state_ret, state_conv, norms, w_ffn1_up, w_ffn1_down, w_in, conv_w,
              w_ret_out, w_conv_out, w_o, w_ffn2_up, w_ffn2_down):
    pos_p = jnp.arange(SEQ, dtype=jnp.int32)
    pos_s = PAST_LEN + jnp.arange(DEC_SEQ, dtype=jnp.int32)
    yp, ys = x_prompt, x_sample
    sp_list, bp_list, ss_list, bs_list = [], [], [], []
    for l in range(DEPTH):
        params = (norms[l], w_ffn1_up[l], w_ffn1_down[l], w_in[l], conv_w[l], w_ret_out[l], w_conv_out[l],
                  w_o[l], w_ffn2_up[l], w_ffn2_down[l])
        s0_p = jnp.zeros((BATCH, N_RET_HEADS, HD_QK, HD_V), x_prompt.dtype)
        b0_p = jnp.zeros((BATCH, CONV_W - 1, D_CONV), x_prompt.dtype)
        yp, sp, bp = layer(yp, pos_p, s0_p, b0_p, *params)
        ys, ss, bs = layer(ys, pos_s, state_ret[l], state_conv[l], *params)
        sp_list.append(sp); bp_list.append(bp); ss_list.append(ss); bs_list.append(bs)
    ret_state_prompt = jnp.stack(sp_list)
    conv_state_prompt = jnp.stack(bp_list)
    ret_state_sample = jnp.stack(ss_list)
    conv_state_sample = jnp.stack(bs_list)
    return (yp, ys, ret_state_prompt, conv_state_prompt, ret_state_sample, conv_state_sample)
```

```python
import functools
import math

import jax
import jax.numpy as jnp
from jax import lax
from jax.experimental import pallas as pl
from jax.experimental.pallas import tpu as pltpu

D_MODEL = 1024
N_HEADS = 4
HD_QK = 128
HD_V = 256
D_QK = N_HEADS * HD_QK
D_V = N_HEADS * HD_V
D_CONV = 1024
CONV_W = 3
D_FF = 2816
CHUNK = 128
ROPE_BASE = 10000.0
NORM_EPS = 1e-6
PAST_LEN = 16384
K_SCALE = HD_QK ** -0.5

_OFF_Q, _OFF_K, _OFF_V, _OFF_G, _OFF_BG, _OFF_CG, _OFF_XC, _OFF_GATES = (
    0, 512, 1024, 2048, 3072, 4096, 5120, 6144)
N_IN = 8192

VMEM_LIMIT_BYTES = 56 * 1024 * 1024
FF_COLS = 256
SEG_COLS = 512
SAMPLE_PAIR_ROWS = 8

F32 = jnp.float32
BF16 = jnp.bfloat16


def _rms(x, g):
    ms = jnp.mean(x * x, axis=-1, keepdims=True)
    return x * lax.rsqrt(ms + NORM_EPS) * g


def _dot(a, b):
    return jnp.dot(a, b, preferred_element_type=F32)


def _dot_nt(a, b):
    return lax.dot_general(a, b, (((1,), (1,)), ((), ())), preferred_element_type=F32)


def _dot_tn(a, b):
    return lax.dot_general(a, b, (((0,), (0,)), ((), ())), preferred_element_type=F32)


def _const_spec(shape):
    return pl.BlockSpec(shape, lambda *_: (0,) * len(shape), pipeline_mode=pl.Buffered(1))


def _params(n_axes):
    return pltpu.CompilerParams(
        dimension_semantics=("arbitrary",) * n_axes, vmem_limit_bytes=VMEM_LIMIT_BYTES)


def _ffn_kernel(x_ref, nin_ref, nout_ref, wup_ref, wdn_ref, o_ref, hid_ref):
    x = x_ref[...]
    u = _rms(x, nin_ref[...]).astype(BF16)
    for c in range(D_FF // FF_COLS):
        lo = c * FF_COLS
        gate = _dot(u, wup_ref[:, lo:lo + FF_COLS])
        up = _dot(u, wup_ref[:, D_FF + lo:D_FF + lo + FF_COLS])
        hid_ref[:, lo:lo + FF_COLS] = (gate * jax.nn.sigmoid(gate) * up).astype(BF16)
    y = _dot(hid_ref[...], wdn_ref[...])
    o_ref[...] = x + 0.5 * _rms(y, nout_ref[...])


def _ffn(x, n_in, n_out, w_up, w_down, tm):
    rows = x.shape[0]
    row_spec = pl.BlockSpec((tm, D_MODEL), lambda i: (i, 0))
    return pl.pallas_call(
        _ffn_kernel,
        grid=(rows // tm,),
        in_specs=[row_spec, _const_spec((1, D_MODEL)), _const_spec((1, D_MODEL)),
                  _const_spec((D_MODEL, 2 * D_FF)), _const_spec((D_FF, D_MODEL))],
        out_specs=row_spec,
        out_shape=jax.ShapeDtypeStruct((rows, D_MODEL), F32),
        scratch_shapes=[pltpu.VMEM((tm, D_FF), BF16)],
        compiler_params=_params(1),
        name="ffn",
    )(x, n_in, n_out, w_up, w_down)


def _inproj_kernel(h_ref, n_ref, w_ref, cos_ref, sin_ref, zeta_ref,
                   q_ref, k_ref, kz_ref, v_ref, g_ref, bg_ref, a_ref, gates_ref):
    u = _rms(h_ref[...], n_ref[...]).astype(BF16)

    def seg(lo):
        return _dot(u, w_ref[:, lo:lo + SEG_COLS])

    cos = cos_ref[...]
    sin = sin_ref[...]

    def rotary(x):
        return x * cos + pltpu.roll(x, HD_QK // 2, 1) * sin

    q = seg(_OFF_Q)
    k = seg(_OFF_K)
    for h in range(N_HEADS):
        cols = slice(h * HD_QK, (h + 1) * HD_QK)
        q_ref[:, cols] = rotary(q[:, cols]).astype(q_ref.dtype)
        kh = rotary(k[:, cols]) * K_SCALE
        k_ref[:, cols] = kh.astype(k_ref.dtype)
        kz_ref[:, cols] = (kh * zeta_ref[:, cols]).astype(kz_ref.dtype)
    for c in range(D_V // SEG_COLS):
        cols = slice(c * SEG_COLS, (c + 1) * SEG_COLS)
        v_ref[:, cols] = seg(_OFF_V + c * SEG_COLS).astype(v_ref.dtype)
        g_ref[:, cols] = seg(_OFF_G + c * SEG_COLS)
        bg_ref[:, cols] = seg(_OFF_BG + c * SEG_COLS)
        a_ref[:, cols] = seg(_OFF_CG + c * SEG_COLS) * seg(_OFF_XC + c * SEG_COLS)
    for c in range(2 * D_MODEL // SEG_COLS):
        cols = slice(c * SEG_COLS, (c + 1) * SEG_COLS)
        gates_ref[:, cols] = seg(_OFF_GATES + c * SEG_COLS)


def _inproj(h, n2, w_in, cos2, sin2, zeta, tm, pos_tiles, qkv_dtype):
    rows = h.shape[0]

    def row_spec(cols):
        return pl.BlockSpec((tm, cols), lambda i: (i, 0))

    pos_spec = pl.BlockSpec((tm, HD_QK), lambda i: (i % pos_tiles, 0))
    out_cols = (D_QK, D_QK, D_QK, D_V, D_V, D_CONV, D_CONV, 2 * D_MODEL)
    out_dtypes = (qkv_dtype,) * 4 + (F32,) * 4
    return pl.pallas_call(
        _inproj_kernel,
        grid=(rows // tm,),
        in_specs=[row_spec(D_MODEL), _const_spec((1, D_MODEL)), _const_spec((D_MODEL, N_IN)),
                  pos_spec, pos_spec, _const_spec((tm, D_QK))],
        out_specs=[row_spec(c) for c in out_cols],
        out_shape=[jax.ShapeDtypeStruct((rows, c), dt) for c, dt in zip(out_cols, out_dtypes)],
        compiler_params=_params(1),
        name="inproj",
    )(h, n2, w_in, cos2, sin2, zeta)


def _head_out(o, g):
    on = o * lax.rsqrt(jnp.mean(o * o, axis=-1, keepdims=True) + NORM_EPS)
    return g * jax.nn.sigmoid(g) * on


def _merge_and_project(h, ret_in, conv_in, gates, n3, wr_ref, wc_ref, wo_ref):
    o_ret = _dot(ret_in, wr_ref[...])
    o_conv = _dot(conv_in, wc_ref[...])
    merged = (jax.nn.sigmoid(gates[:, :D_MODEL]) * o_ret
              + jax.nn.sigmoid(gates[:, D_MODEL:]) * o_conv)
    m = _dot(merged.astype(BF16), wo_ref[...])
    return h + _rms(m, n3)


def _mixer_prompt_kernel(gch_ref, h_ref, q_ref, k_ref, kz_ref, v_ref, g_ref, bg_ref, a_ref,
                         gates_ref, dmat_ref, xi_ref, cw_ref, n3_ref, wr_ref, wc_ref, wo_ref,
                         hout_ref, s_ref, halo_ref, ret_ref):
    tm = h_ref.shape[0]
    halo = CONV_W - 1
    base = 8

    @pl.when(pl.program_id(1) == 0)
    def _():
        s_ref[...] = jnp.zeros_like(s_ref)
        halo_ref[0:base, :] = jnp.zeros((base, D_CONV), F32)

    a = a_ref[...]
    halo_ref[base:base + tm, :] = a
    z = (cw_ref[0:1, :] * halo_ref[base - 2:base - 2 + tm, :]
         + cw_ref[1:2, :] * halo_ref[base - 1:base - 1 + tm, :]
         + cw_ref[2:3, :] * a)
    conv_in = (bg_ref[...] * z).astype(BF16)
    halo_ref[base - halo:base, :] = a[tm - halo:tm, :]

    for c in range(tm // CHUNK):
        rows = slice(c * CHUNK, (c + 1) * CHUNK)
        for hh in range(N_HEADS):
            qk_cols = slice(hh * HD_QK, (hh + 1) * HD_QK)
            v_cols = slice(hh * HD_V, (hh + 1) * HD_V)
            qc = q_ref[rows, qk_cols]
            vc = v_ref[rows, v_cols]
            scores = _dot_nt(qc, k_ref[rows, qk_cols]) * dmat_ref[hh]
            inner = _dot(scores.astype(BF16), vc)
            s = s_ref[0, hh]
            cross = _dot(qc, s.astype(BF16)) * xi_ref[hh]
            s_ref[0, hh] = gch_ref[hh] * s + _dot_tn(kz_ref[rows, qk_cols], vc)
            ret_ref[rows, v_cols] = _head_out(inner + cross, g_ref[rows, v_cols]).astype(BF16)

    hout_ref[...] = _merge_and_project(h_ref[...], ret_ref[...], conv_in, gates_ref[...],
                                       n3_ref[...], wr_ref, wc_ref, wo_ref)


def _mixer_prompt(gch, h, q, k, kz, v, g, bg, a, gates, dmat, xi, conv_w, n3, wr, wc, wo,
                  batch, seq, tm):
    nt = seq // tm

    def row_spec(cols):
        return pl.BlockSpec((tm, cols), lambda b, t: (b * nt + t, 0))

    w_spec = _const_spec((D_MODEL, D_MODEL))
    return pl.pallas_call(
        _mixer_prompt_kernel,
        grid=(batch, nt),
        in_specs=[pl.BlockSpec(memory_space=pltpu.SMEM),
                  row_spec(D_MODEL), row_spec(D_QK), row_spec(D_QK), row_spec(D_QK),
                  row_spec(D_V), row_spec(D_V), row_spec(D_CONV), row_spec(D_CONV),
                  row_spec(2 * D_MODEL),
                  _const_spec((N_HEADS, CHUNK, CHUNK)), _const_spec((N_HEADS, CHUNK, HD_V)),
                  _const_spec((CONV_W, D_CONV)), _const_spec((1, D_MODEL)),
                  w_spec, w_spec, w_spec],
        out_specs=[row_spec(D_MODEL),
                   pl.BlockSpec((1, N_HEADS, HD_QK, HD_V), lambda b, t: (b, 0, 0, 0))],
        out_shape=[jax.ShapeDtypeStruct((batch * seq, D_MODEL), F32),
                   jax.ShapeDtypeStruct((batch, N_HEADS, HD_QK, HD_V), F32)],
        scratch_shapes=[pltpu.VMEM((tm + 8, D_CONV), F32), pltpu.VMEM((tm, D_V), BF16)],
        compiler_params=_params(2),
        name="mixer_prompt",
    )(gch, h, q, k, kz, v, g, bg, a, gates, dmat, xi, conv_w, n3, wr, wc, wo)


def _mixer_sample_kernel(gch_ref, h_ref, q_ref, k_ref, kz_ref, v_ref, g_ref, bg_ref, a_ref,
                         e1_ref, e2_ref, gates_ref, s_in_ref, dmat_ref, xi_ref, cw_ref, n3_ref,
                         wr_ref, wc_ref, wo_ref, hout_ref, s_out_ref, ret_ref):
    rows = h_ref.shape[0]
    dec_seq = SAMPLE_PAIR_ROWS // 2
    n_pairs = rows // SAMPLE_PAIR_ROWS

    a = a_ref[...]
    tok = lax.broadcasted_iota(jnp.int32, (rows, D_CONV), 0) % dec_seq
    prev1 = jnp.where(tok >= 1, pltpu.roll(a, 1, 0), e1_ref[...])
    prev2 = jnp.where(tok >= 2, pltpu.roll(a, 2, 0), e2_ref[...])
    z = cw_ref[0:1, :] * prev2 + cw_ref[1:2, :] * prev1 + cw_ref[2:3, :] * a
    conv_in = (bg_ref[...] * z).astype(BF16)

    first_qk = lax.broadcasted_iota(jnp.int32, (SAMPLE_PAIR_ROWS, HD_QK), 0) < dec_seq
    first_v = lax.broadcasted_iota(jnp.int32, (SAMPLE_PAIR_ROWS, HD_V), 0) < dec_seq

    def pair_body(p, carry):
        r0 = pl.multiple_of(p * SAMPLE_PAIR_ROWS, SAMPLE_PAIR_ROWS)
        prow = pl.ds(r0, SAMPLE_PAIR_ROWS)
        for hh in range(N_HEADS):
            qk_cols = slice(hh * HD_QK, (hh + 1) * HD_QK)
            v_cols = slice(hh * HD_V, (hh + 1) * HD_V)
            q8 = q_ref[prow, qk_cols].astype(BF16)
            k8 = k_ref[prow, qk_cols].astype(BF16)
            kz8 = kz_ref[prow, qk_cols]
            v8 = v_ref[prow, v_cols].astype(BF16)
            scores = _dot_nt(q8, k8) * dmat_ref[hh]
            inner = _dot(scores.astype(BF16), v8)
            s0 = s_in_ref[2 * p, hh]
            s1 = s_in_ref[2 * p + 1, hh]
            cross = jnp.where(first_v, _dot(q8, s0.astype(BF16)), _dot(q8, s1.astype(BF16)))
            cross = cross * xi_ref[hh]
            kz0 = jnp.where(first_qk, kz8, 0.0).astype(BF16)
            kz1 = jnp.where(first_qk, 0.0, kz8).astype(BF16)
            s_out_ref[2 * p, hh] = gch_ref[hh] * s0 + _dot_tn(kz0, v8)
            s_out_ref[2 * p + 1, hh] = gch_ref[hh] * s1 + _dot_tn(kz1, v8)
            ret_ref[prow, v_cols] = _head_out(inner + cross, g_ref[prow, v_cols])
        return carry

    lax.fori_loop(0, n_pairs, pair_body, 0)

    hout_ref[...] = _merge_and_project(h_ref[...], ret_ref[...].astype(BF16), conv_in,
                                       gates_ref[...], n3_ref[...], wr_ref, wc_ref, wo_ref)


def _mixer_sample(gch, h, q, k, kz, v, g, bg, a, e1, e2, gates, state_ret, layer, dmat, xi,
                  conv_w, n3, wr, wc, wo, bb):
    dec_batch = state_ret.shape[1]
    rows = bb * (SAMPLE_PAIR_ROWS // 2)

    def row_spec(cols):
        return pl.BlockSpec((rows, cols), lambda i: (i, 0))

    w_spec = _const_spec((D_MODEL, D_MODEL))
    return pl.pallas_call(
        _mixer_sample_kernel,
        grid=(dec_batch // bb,),
        in_specs=[pl.BlockSpec(memory_space=pltpu.SMEM),
                  row_spec(D_MODEL), row_spec(D_QK), row_spec(D_QK), row_spec(D_QK),
                  row_spec(D_V), row_spec(D_V), row_spec(D_CONV), row_spec(D_CONV),
                  row_spec(D_CONV), row_spec(D_CONV), row_spec(2 * D_MODEL),
                  pl.BlockSpec((None, bb, N_HEADS, HD_QK, HD_V), lambda i: (layer, i, 0, 0, 0)),
                  _const_spec((N_HEADS, SAMPLE_PAIR_ROWS, SAMPLE_PAIR_ROWS)),
                  _const_spec((N_HEADS, SAMPLE_PAIR_ROWS, HD_V)),
                  _const_spec((CONV_W, D_CONV)), _const_spec((1, D_MODEL)),
                  w_spec, w_spec, w_spec],
        out_specs=[row_spec(D_MODEL),
                   pl.BlockSpec((bb, N_HEADS, HD_QK, HD_V), lambda i: (i, 0, 0, 0))],
        out_shape=[jax.ShapeDtypeStruct(h.shape, F32),
                   jax.ShapeDtypeStruct(state_ret.shape[1:], F32)],
        scratch_shapes=[pltpu.VMEM((rows, D_V), F32)],
        compiler_params=_params(1),
        name="mixer_sample",
    )(gch, h, q, k, kz, v, g, bg, a, e1, e2, gates, state_ret, dmat, xi, conv_w, n3, wr, wc, wo)


def _log_gammas():
    gam = 1.0 - jnp.exp(jnp.linspace(math.log(1.0 / 32), math.log(1.0 / 512), N_HEADS, dtype=F32))
    return jnp.log(gam)


def _rotary_tables(pos):
    half = HD_QK // 2
    inv_freq = ROPE_BASE ** (-(jnp.arange(half, dtype=F32) * 2.0 / HD_QK))
    ang = pos.astype(F32)[:, None] * inv_freq[None, :]
    cos, sin = jnp.cos(ang), jnp.sin(ang)
    return jnp.concatenate([cos, cos], axis=-1), jnp.concatenate([-sin, sin], axis=-1)


def _decay_tables(c, lg):
    idx = jnp.arange(c, dtype=F32)
    diff = idx[:, None] - idx[None, :]
    dmat = jnp.where(diff[None] >= 0, jnp.exp(jnp.maximum(diff, 0.0)[None] * lg[:, None, None]), 0.0)
    xi = jnp.exp((idx[:, None] + 1.0) * lg[None, :])
    zeta = jnp.exp((c - 1.0 - idx)[:, None] * lg[None, :])
    return dmat, xi.T, zeta, jnp.exp(c * lg)


def _lanes(t, width):
    return jnp.repeat(t, width, axis=1)


def kernel(x_prompt, x_sample, state_ret, state_conv, norms, w_ffn1_up, w_ffn1_down, w_in, conv_w,
           w_ret_out, w_conv_out, w_o, w_ffn2_up, w_ffn2_down):
    batch, seq, _ = x_prompt.shape
    dec_batch, dec_seq, _ = x_sample.shape
    depth = norms.shape[0]
    assert seq % CHUNK == 0 and 2 * dec_seq == SAMPLE_PAIR_ROWS and CHUNK % dec_seq == 0
    tm_p = 512
    tm_mix = 256
    n_s = dec_batch * dec_seq
    bb = 8

    lg = _log_gammas()
    cos_p, sin_p = _rotary_tables(jnp.arange(seq, dtype=jnp.int32))
    dmat_p, xi_p, zeta_p, gch_p = _decay_tables(CHUNK, lg)
    zeta_p = jnp.tile(_lanes(zeta_p, HD_QK), (tm_p // CHUNK, 1))
    xi_p = jnp.broadcast_to(xi_p[:, :, None], (N_HEADS, CHUNK, HD_V))
    cos_s, sin_s = _rotary_tables(PAST_LEN + jnp.arange(dec_seq, dtype=jnp.int32))
    cos_s = jnp.tile(cos_s, (dec_batch, 1))
    sin_s = jnp.tile(sin_s, (dec_batch, 1))
    dmat_s, xi_s, zeta_s, gch_s = _decay_tables(dec_seq, lg)
    zeta_s = jnp.tile(_lanes(zeta_s, HD_QK), (dec_batch, 1))
    eye2 = jnp.eye(2, dtype=F32)
    dmat_s = jnp.einsum("pq,hij->hpiqj", eye2, dmat_s).reshape(
        N_HEADS, SAMPLE_PAIR_ROWS, SAMPLE_PAIR_ROWS)
    xi_s = jnp.broadcast_to(jnp.tile(xi_s, (1, 2))[:, :, None], (N_HEADS, SAMPLE_PAIR_ROWS, HD_V))

    hp = x_prompt.reshape(batch * seq, D_MODEL)
    hs = x_sample.reshape(n_s, D_MODEL)
    sp_list, bp_list, ss_list, bs_list = [], [], [], []
    for l in range(depth):
        n = [norms[l, i][None, :] for i in range(norms.shape[1])]
        wu1, wd1 = w_ffn1_up[l].astype(BF16), w_ffn1_down[l].astype(BF16)
        wu2, wd2 = w_ffn2_up[l].astype(BF16), w_ffn2_down[l].astype(BF16)
        wi = w_in[l].astype(BF16)
        wr, wc, wo = (w_ret_out[l].astype(BF16), w_conv_out[l].astype(BF16), w_o[l].astype(BF16))
        cw = conv_w[l]

        hp = _ffn(hp, n[0], n[1], wu1, wd1, tm_p)
        hs = _ffn(hs, n[0], n[1], wu1, wd1, n_s)

        qp, kp, kzp, vp, gp, bgp, ap, gatesp = _inproj(
            hp, n[2], wi, cos_p, sin_p, zeta_p, tm_p, seq // tm_p, BF16)
        qs, ks, kzs, vs, gs, bgs, a_s, gatess = _inproj(
            hs, n[2], wi, cos_s, sin_s, zeta_s, n_s, 1, F32)

        hp, sp = _mixer_prompt(gch_p, hp, qp, kp, kzp, vp, gp, bgp, ap, gatesp, dmat_p, xi_p,
                               cw, n[3], wr, wc, wo, batch, seq, tm_mix)
        buf = state_conv[l]
        e2 = jnp.pad(buf, ((0, 0), (0, dec_seq - (CONV_W - 1)), (0, 0))).reshape(n_s, D_CONV)
        e1 = jnp.pad(buf[:, 1:], ((0, 0), (0, dec_seq - 1), (0, 0))).reshape(n_s, D_CONV)
        hs, ss = _mixer_sample(gch_s, hs, qs, ks, kzs, vs, gs, bgs, a_s, e1, e2, gatess,
                               state_ret, l, dmat_s, xi_s, cw, n[3], wr, wc, wo, bb)

        hp = _ffn(hp, n[4], n[5], wu2, wd2, tm_p)
        hs = _ffn(hs, n[4], n[5], wu2, wd2, n_s)

        sp_list.append(sp)
        ss_list.append(ss)
        bp_list.append(ap.reshape(batch, seq, D_CONV)[:, seq - (CONV_W - 1):])
        bs_list.append(a_s.reshape(dec_batch, dec_seq, D_CONV)[:, dec_seq - (CONV_W - 1):])

    return (hp.reshape(batch, seq, D_MODEL), hs.reshape(dec_batch, dec_seq, D_MODEL),
            jnp.stack(sp_list), jnp.stack(bp_list), jnp.stack(ss_list), jnp.stack(bs_list))
```

```python
import functools
import math

import jax
import jax.numpy as jnp
from jax import lax
from jax.experimental import pallas as pl
from jax.experimental.pallas import tpu as pltpu

D_MODEL = 1024
N_HEADS = 4
HD_QK = 128
HD_V = 256
D_QK = N_HEADS * HD_QK
D_V = N_HEADS * HD_V
D_CONV = 1024
CONV_W = 3
D_FF = 2816
CHUNK = 128
ROPE_BASE = 10000.0
NORM_EPS = 1e-6
PAST_LEN = 16384
K_SCALE = HD_QK ** -0.5

_OFF_Q, _OFF_K, _OFF_V, _OFF_G, _OFF_BG, _OFF_CG, _OFF_XC, _OFF_GATES = (
    0, 512, 1024, 2048, 3072, 4096, 5120, 6144)
N_IN = 8192

VMEM_LIMIT_BYTES = 56 * 1024 * 1024
FF_COLS = 256
SEG_COLS = 512
SAMPLE_PAIR_ROWS = 8
HALO_BASE = 8

F32 = jnp.float32
BF16 = jnp.bfloat16


def _rms(x, g):
    ms = jnp.mean(x * x, axis=-1, keepdims=True)
    return x * lax.rsqrt(ms + NORM_EPS) * g


def _dot(a, b):
    return jnp.dot(a, b, preferred_element_type=F32)


def _dot_nt(a, b):
    return lax.dot_general(a, b, (((1,), (1,)), ((), ())), preferred_element_type=F32)


def _dot_tn(a, b):
    return lax.dot_general(a, b, (((0,), (0,)), ((), ())), preferred_element_type=F32)


def _const_spec(shape):
    return pl.BlockSpec(shape, lambda *_: (0,) * len(shape), pipeline_mode=pl.Buffered(1))


def _params(n_axes):
    return pltpu.CompilerParams(
        dimension_semantics=("arbitrary",) * n_axes, vmem_limit_bytes=VMEM_LIMIT_BYTES)


def _ffn_kernel(x_ref, nin_ref, nout_ref, wup_ref, wdn_ref, o_ref, hid_ref):
    x = x_ref[...]
    u = _rms(x, nin_ref[...]).astype(BF16)
    for c in range(D_FF // FF_COLS):
        lo = c * FF_COLS
        gate = _dot(u, wup_ref[:, lo:lo + FF_COLS])
        up = _dot(u, wup_ref[:, D_FF + lo:D_FF + lo + FF_COLS])
        hid_ref[:, lo:lo + FF_COLS] = (gate * jax.nn.sigmoid(gate) * up).astype(BF16)
    y = _dot(hid_ref[...], wdn_ref[...])
    o_ref[...] = x + 0.5 * _rms(y, nout_ref[...])


def _ffn(x, n_in, n_out, w_up, w_down, tm):
    rows = x.shape[0]
    row_spec = pl.BlockSpec((tm, D_MODEL), lambda i: (i, 0))
    return pl.pallas_call(
        _ffn_kernel,
        grid=(rows // tm,),
        in_specs=[row_spec, _const_spec((1, D_MODEL)), _const_spec((1, D_MODEL)),
                  _const_spec((D_MODEL, 2 * D_FF)), _const_spec((D_FF, D_MODEL))],
        out_specs=row_spec,
        out_shape=jax.ShapeDtypeStruct((rows, D_MODEL), F32),
        scratch_shapes=[pltpu.VMEM((tm, D_FF), BF16)],
        compiler_params=_params(1),
        name="ffn",
    )(x, n_in, n_out, w_up, w_down)


def _project_qk(seg, cos, sin, zeta_ref, q_ref, k_ref, kz_ref):
    def rotary(x):
        return x * cos + pltpu.roll(x, HD_QK // 2, 1) * sin

    q = seg(_OFF_Q)
    k = seg(_OFF_K)
    for h in range(N_HEADS):
        cols = slice(h * HD_QK, (h + 1) * HD_QK)
        q_ref[:, cols] = rotary(q[:, cols]).astype(q_ref.dtype)
        kh = rotary(k[:, cols]) * K_SCALE
        k_ref[:, cols] = kh.astype(k_ref.dtype)
        kz_ref[:, cols] = (kh * zeta_ref[:, cols]).astype(kz_ref.dtype)


def _head_out(o, g):
    on = o * lax.rsqrt(jnp.mean(o * o, axis=-1, keepdims=True) + NORM_EPS)
    return g * jax.nn.sigmoid(g) * on


def _merge_and_project(h, ret_in, conv_in, gate_r, gate_c, n3, wr_ref, wc_ref, wo_ref):
    merged = gate_r * _dot(ret_in, wr_ref[...]) + gate_c * _dot(conv_in, wc_ref[...])
    m = _dot(merged.astype(BF16), wo_ref[...])
    return h + _rms(m, n3)


def _mixer_prompt_kernel(gch_ref, h_ref, nin_ref, win_ref, cos_ref, sin_ref, zeta_ref,
                         dmat_ref, xi_ref, cw_ref, n3_ref, wr_ref, wc_ref, wo_ref,
                         hout_ref, s_ref, cs_ref,
                         q_sc, k_sc, kz_sc, v_sc, g_sc, halo_sc, conv_sc, gate_sc, ret_sc):
    tm = h_ref.shape[0]
    halo = CONV_W - 1
    base = HALO_BASE

    @pl.when(pl.program_id(1) == 0)
    def _():
        s_ref[...] = jnp.zeros_like(s_ref)
        halo_sc[0:base, :] = jnp.zeros((base, D_CONV), F32)

    x = h_ref[...]
    u = _rms(x, nin_ref[...]).astype(BF16)

    def seg(lo):
        return _dot(u, win_ref[:, lo:lo + SEG_COLS])

    _project_qk(seg, cos_ref[...], sin_ref[...], zeta_ref, q_sc, k_sc, kz_sc)
    for c in range(D_V // SEG_COLS):
        cols = slice(c * SEG_COLS, (c + 1) * SEG_COLS)
        v_sc[:, cols] = seg(_OFF_V + c * SEG_COLS).astype(BF16)
        g_sc[:, cols] = seg(_OFF_G + c * SEG_COLS)
        a = seg(_OFF_CG + c * SEG_COLS) * seg(_OFF_XC + c * SEG_COLS)
        halo_sc[base:base + tm, cols] = a
        z = (cw_ref[0:1, cols] * halo_sc[base - 2:base - 2 + tm, cols]
             + cw_ref[1:2, cols] * halo_sc[base - 1:base - 1 + tm, cols]
             + cw_ref[2:3, cols] * a)
        conv_sc[:, cols] = (seg(_OFF_BG + c * SEG_COLS) * z).astype(BF16)
        halo_sc[base - halo:base, cols] = a[tm - halo:tm, :]
        cs_ref[0, :, cols] = a[tm - halo:tm, :]
    for c in range(2 * D_MODEL // SEG_COLS):
        cols = slice(c * SEG_COLS, (c + 1) * SEG_COLS)
        gate_sc[:, cols] = jax.nn.sigmoid(seg(_OFF_GATES + c * SEG_COLS))

    for c in range(tm // CHUNK):
        rows = slice(c * CHUNK, (c + 1) * CHUNK)
        for hh in range(N_HEADS):
            qk_cols = slice(hh * HD_QK, (hh + 1) * HD_QK)
            v_cols = slice(hh * HD_V, (hh + 1) * HD_V)
            qc = q_sc[rows, qk_cols]
            vc = v_sc[rows, v_cols]
            scores = _dot_nt(qc, k_sc[rows, qk_cols]) * dmat_ref[hh]
            inner = _dot(scores.astype(BF16), vc)
            s = s_ref[0, hh]
            cross = _dot(qc, s.astype(BF16)) * xi_ref[hh]
            s_ref[0, hh] = gch_ref[hh] * s + _dot_tn(kz_sc[rows, qk_cols], vc)
            ret_sc[rows, v_cols] = _head_out(inner + cross, g_sc[rows, v_cols]).astype(BF16)

    hout_ref[...] = _merge_and_project(
        x, ret_sc[...], conv_sc[...], gate_sc[:, :D_MODEL], gate_sc[:, D_MODEL:],
        n3_ref[...], wr_ref, wc_ref, wo_ref)


def _mixer_prompt(gch, h, n2, w_in, cos2, sin2, zeta, dmat, xi, conv_w, n3, wr, wc, wo,
                  batch, seq, tm):
    nt = seq // tm
    row_spec = pl.BlockSpec((tm, D_MODEL), lambda b, t: (b * nt + t, 0))
    pos_spec = pl.BlockSpec((tm, HD_QK), lambda b, t: (t, 0))
    w_spec = _const_spec((D_MODEL, D_MODEL))
    return pl.pallas_call(
        _mixer_prompt_kernel,
        grid=(batch, nt),
        in_specs=[pl.BlockSpec(memory_space=pltpu.SMEM),
                  row_spec, _const_spec((1, D_MODEL)), _const_spec((D_MODEL, N_IN)),
                  pos_spec, pos_spec, _const_spec((tm, D_QK)),
                  _const_spec((N_HEADS, CHUNK, CHUNK)), _const_spec((N_HEADS, CHUNK, HD_V)),
                  _const_spec((CONV_W, D_CONV)), _const_spec((1, D_MODEL)),
                  w_spec, w_spec, w_spec],
        out_specs=[row_spec,
                   pl.BlockSpec((1, N_HEADS, HD_QK, HD_V), lambda b, t: (b, 0, 0, 0)),
                   pl.BlockSpec((1, CONV_W - 1, D_CONV), lambda b, t: (b, 0, 0))],
        out_shape=[jax.ShapeDtypeStruct((batch * seq, D_MODEL), F32),
                   jax.ShapeDtypeStruct((batch, N_HEADS, HD_QK, HD_V), F32),
                   jax.ShapeDtypeStruct((batch, CONV_W - 1, D_CONV), F32)],
        scratch_shapes=[pltpu.VMEM((tm, D_QK), BF16), pltpu.VMEM((tm, D_QK), BF16),
                        pltpu.VMEM((tm, D_QK), BF16), pltpu.VMEM((tm, D_V), BF16),
                        pltpu.VMEM((tm, D_V), F32), pltpu.VMEM((tm + HALO_BASE, D_CONV), F32),
                        pltpu.VMEM((tm, D_CONV), BF16), pltpu.VMEM((tm, 2 * D_MODEL), F32),
                        pltpu.VMEM((tm, D_V), BF16)],
        compiler_params=_params(2),
        name="mixer_prompt",
    )(gch, h, n2, w_in, cos2, sin2, zeta, dmat, xi, conv_w, n3, wr, wc, wo)


def _inproj_kernel(h_ref, n_ref, w_ref, cos_ref, sin_ref, zeta_ref,
                   q_ref, k_ref, kz_ref, v_ref, g_ref, bg_ref, a_ref, gates_ref):
    u = _rms(h_ref[...], n_ref[...]).astype(BF16)

    def seg(lo):
        return _dot(u, w_ref[:, lo:lo + SEG_COLS])

    _project_qk(seg, cos_ref[...], sin_ref[...], zeta_ref, q_ref, k_ref, kz_ref)
    for c in range(D_V // SEG_COLS):
        cols = slice(c * SEG_COLS, (c + 1) * SEG_COLS)
        v_ref[:, cols] = seg(_OFF_V + c * SEG_COLS)
        g_ref[:, cols] = seg(_OFF_G + c * SEG_COLS)
        bg_ref[:, cols] = seg(_OFF_BG + c * SEG_COLS)
        a_ref[:, cols] = seg(_OFF_CG + c * SEG_COLS) * seg(_OFF_XC + c * SEG_COLS)
    for c in range(2 * D_MODEL // SEG_COLS):
        cols = slice(c * SEG_COLS, (c + 1) * SEG_COLS)
        gates_ref[:, cols] = seg(_OFF_GATES + c * SEG_COLS)


def _inproj(h, n2, w_in, cos2, sin2, zeta):
    rows = h.shape[0]

    def full(shape):
        return pl.BlockSpec(shape, lambda i: (0, 0))

    args = (h, n2, w_in, cos2, sin2, zeta)
    out_cols = (D_QK, D_QK, D_QK, D_V, D_V, D_CONV, D_CONV, 2 * D_MODEL)
    return pl.pallas_call(
        _inproj_kernel,
        grid=(1,),
        in_specs=[full(x.shape) for x in args],
        out_specs=[full((rows, c)) for c in out_cols],
        out_shape=[jax.ShapeDtypeStruct((rows, c), F32) for c in out_cols],
        compiler_params=_params(1),
        name="inproj",
    )(*args)


def _mixer_sample_kernel(gch_ref, h_ref, q_ref, k_ref, kz_ref, v_ref, g_ref, bg_ref, a_ref,
                         e1_ref, e2_ref, gates_ref, s_in_ref, dmat_ref, xi_ref, cw_ref, n3_ref,
                         wr_ref, wc_ref, wo_ref, *rest, n_state_copies):
    hout_ref, s_out_ref, ret_ref = rest[-3:]
    rows = h_ref.shape[0]
    dec_seq = SAMPLE_PAIR_ROWS // 2
    n_pairs = rows // SAMPLE_PAIR_ROWS

    a = a_ref[...]
    tok = lax.broadcasted_iota(jnp.int32, (rows, D_CONV), 0) % dec_seq
    prev1 = jnp.where(tok >= 1, pltpu.roll(a, 1, 0), e1_ref[...])
    prev2 = jnp.where(tok >= 2, pltpu.roll(a, 2, 0), e2_ref[...])
    z = cw_ref[0:1, :] * prev2 + cw_ref[1:2, :] * prev1 + cw_ref[2:3, :] * a
    conv_in = (bg_ref[...] * z).astype(BF16)

    first_qk = lax.broadcasted_iota(jnp.int32, (SAMPLE_PAIR_ROWS, HD_QK), 0) < dec_seq
    first_v = lax.broadcasted_iota(jnp.int32, (SAMPLE_PAIR_ROWS, HD_V), 0) < dec_seq

    def store_state(b, hh, val):
        if n_state_copies == 1:
            s_out_ref[b, hh] = val
        else:
            for l in range(n_state_copies):
                s_out_ref[l, b, hh] = val

    def pair_body(p, carry):
        r0 = pl.multiple_of(p * SAMPLE_PAIR_ROWS, SAMPLE_PAIR_ROWS)
        prow = pl.ds(r0, SAMPLE_PAIR_ROWS)
        for hh in range(N_HEADS):
            qk_cols = slice(hh * HD_QK, (hh + 1) * HD_QK)
            v_cols = slice(hh * HD_V, (hh + 1) * HD_V)
            q8 = q_ref[prow, qk_cols].astype(BF16)
            k8 = k_ref[prow, qk_cols].astype(BF16)
            kz8 = kz_ref[prow, qk_cols]
            v8 = v_ref[prow, v_cols].astype(BF16)
            scores = _dot_nt(q8, k8) * dmat_ref[hh]
            inner = _dot(scores.astype(BF16), v8)
            s0 = s_in_ref[2 * p, hh]
            s1 = s_in_ref[2 * p + 1, hh]
            cross = jnp.where(first_v, _dot(q8, s0.astype(BF16)), _dot(q8, s1.astype(BF16)))
            cross = cross * xi_ref[hh]
            kz0 = jnp.where(first_qk, kz8, 0.0).astype(BF16)
            kz1 = jnp.where(first_qk, 0.0, kz8).astype(BF16)
            store_state(2 * p, hh, gch_ref[hh] * s0 + _dot_tn(kz0, v8))
            store_state(2 * p + 1, hh, gch_ref[hh] * s1 + _dot_tn(kz1, v8))
            ret_ref[prow, v_cols] = _head_out(inner + cross, g_ref[prow, v_cols])
        return carry

    lax.fori_loop(0, n_pairs, pair_body, 0)

    gates = gates_ref[...]
    hout_ref[...] = _merge_and_project(
        h_ref[...], ret_ref[...].astype(BF16), conv_in,
        jax.nn.sigmoid(gates[:, :D_MODEL]), jax.nn.sigmoid(gates[:, D_MODEL:]),
        n3_ref[...], wr_ref, wc_ref, wo_ref)


def _mixer_sample(gch, h, q, k, kz, v, g, bg, a, e1, e2, gates, state_ret, layer, states_so_far,
                  dmat, xi, conv_w, n3, wr, wc, wo, bb):
    depth, dec_batch = state_ret.shape[:2]
    rows = bb * (SAMPLE_PAIR_ROWS // 2)

    def row_spec(cols):
        return pl.BlockSpec((rows, cols), lambda i: (i, 0))

    w_spec = _const_spec((D_MODEL, D_MODEL))
    state_block = (bb, N_HEADS, HD_QK, HD_V)
    in_specs = [pl.BlockSpec(memory_space=pltpu.SMEM),
                row_spec(D_MODEL), row_spec(D_QK), row_spec(D_QK), row_spec(D_QK),
                row_spec(D_V), row_spec(D_V), row_spec(D_CONV), row_spec(D_CONV),
                row_spec(D_CONV), row_spec(D_CONV), row_spec(2 * D_MODEL),
                pl.BlockSpec((None,) + state_block, lambda i: (layer, i, 0, 0, 0)),
                _const_spec((N_HEADS, SAMPLE_PAIR_ROWS, SAMPLE_PAIR_ROWS)),
                _const_spec((N_HEADS, SAMPLE_PAIR_ROWS, HD_V)),
                _const_spec((CONV_W, D_CONV)), _const_spec((1, D_MODEL)),
                w_spec, w_spec, w_spec]
    args = [gch, h, q, k, kz, v, g, bg, a, e1, e2, gates, state_ret, dmat, xi, conv_w, n3,
            wr, wc, wo]
    if states_so_far is None:
        n_copies = depth
        state_spec = pl.BlockSpec((depth,) + state_block, lambda i: (0, i, 0, 0, 0))
        aliases = {}
    else:
        n_copies = 1
        state_spec = pl.BlockSpec((None,) + state_block, lambda i: (layer, i, 0, 0, 0))
        in_specs.append(pl.BlockSpec(memory_space=pl.ANY))
        args.append(states_so_far)
        aliases = {len(args) - 1: 1}
    return pl.pallas_call(
        functools.partial(_mixer_sample_kernel, n_state_copies=n_copies),
        grid=(dec_batch // bb,),
        in_specs=in_specs,
        out_specs=[row_spec(D_MODEL), state_spec],
        out_shape=[jax.ShapeDtypeStruct(h.shape, F32),
                   jax.ShapeDtypeStruct(state_ret.shape, F32)],
        scratch_shapes=[pltpu.VMEM((rows, D_V), F32)],
        input_output_aliases=aliases,
        compiler_params=_params(1),
        name="mixer_sample",
    )(*args)


def _log_gammas():
    gam = 1.0 - jnp.exp(jnp.linspace(math.log(1.0 / 32), math.log(1.0 / 512), N_HEADS, dtype=F32))
    return jnp.log(gam)


def _rotary_tables(pos):
    half = HD_QK // 2
    inv_freq = ROPE_BASE ** (-(jnp.arange(half, dtype=F32) * 2.0 / HD_QK))
    ang = pos.astype(F32)[:, None] * inv_freq[None, :]
    cos, sin = jnp.cos(ang), jnp.sin(ang)
    return jnp.concatenate([cos, cos], axis=-1), jnp.concatenate([-sin, sin], axis=-1)


def _decay_tables(c, lg):
    idx = jnp.arange(c, dtype=F32)
    diff = idx[:, None] - idx[None, :]
    dmat = jnp.where(diff[None] >= 0, jnp.exp(jnp.maximum(diff, 0.0)[None] * lg[:, None, None]), 0.0)
    xi = jnp.exp((idx[:, None] + 1.0) * lg[None, :])
    zeta = jnp.exp((c - 1.0 - idx)[:, None] * lg[None, :])
    return dmat, xi.T, zeta, jnp.exp(c * lg)


def _lanes(t, width):
    return jnp.repeat(t, width, axis=1)


def kernel(x_prompt, x_sample, state_ret, state_conv, norms, w_ffn1_up, w_ffn1_down, w_in, conv_w,
           w_ret_out, w_conv_out, w_o, w_ffn2_up, w_ffn2_down):
    batch, seq, _ = x_prompt.shape
    dec_batch, dec_seq, _ = x_sample.shape
    depth = norms.shape[0]
    assert seq % CHUNK == 0 and 2 * dec_seq == SAMPLE_PAIR_ROWS and CHUNK % dec_seq == 0
    tm_p = 512
    tm_mix = 512
    n_s = dec_batch * dec_seq
    bb = 8

    lg = _log_gammas()
    cos_p, sin_p = _rotary_tables(jnp.arange(seq, dtype=jnp.int32))
    dmat_p, xi_p, zeta_p, gch_p = _decay_tables(CHUNK, lg)
    zeta_p = jnp.tile(_lanes(zeta_p, HD_QK), (tm_mix // CHUNK, 1))
    xi_p = jnp.broadcast_to(xi_p[:, :, None], (N_HEADS, CHUNK, HD_V))
    cos_s, sin_s = _rotary_tables(PAST_LEN + jnp.arange(dec_seq, dtype=jnp.int32))
    cos_s = jnp.tile(cos_s, (dec_batch, 1))
    sin_s = jnp.tile(sin_s, (dec_batch, 1))
    dmat_s, xi_s, zeta_s, gch_s = _decay_tables(dec_seq, lg)
    zeta_s = jnp.tile(_lanes(zeta_s, HD_QK), (dec_batch, 1))
    eye2 = jnp.eye(2, dtype=F32)
    dmat_s = jnp.einsum("pq,hij->hpiqj", eye2, dmat_s).reshape(
        N_HEADS, SAMPLE_PAIR_ROWS, SAMPLE_PAIR_ROWS)
    xi_s = jnp.broadcast_to(jnp.tile(xi_s, (1, 2))[:, :, None], (N_HEADS, SAMPLE_PAIR_ROWS, HD_V))

    hp = x_prompt.reshape(batch * seq, D_MODEL)
    hs = x_sample.reshape(n_s, D_MODEL)
    sp_list, bp_list, bs_list = [], [], []
    ss_all = None
    for l in range(depth):
        n = [norms[l, i][None, :] for i in range(norms.shape[1])]
        wu1, wd1 = w_ffn1_up[l].astype(BF16), w_ffn1_down[l].astype(BF16)
        wu2, wd2 = w_ffn2_up[l].astype(BF16), w_ffn2_down[l].astype(BF16)
        wi = w_in[l].astype(BF16)
        wr, wc, wo = (w_ret_out[l].astype(BF16), w_conv_out[l].astype(BF16), w_o[l].astype(BF16))
        cw = conv_w[l]

        hp = _ffn(hp, n[0], n[1], wu1, wd1, tm_p)
        hs = _ffn(hs, n[0], n[1], wu1, wd1, n_s)

        hp, sp, bp = _mixer_prompt(gch_p, hp, n[2], wi, cos_p, sin_p, zeta_p, dmat_p, xi_p,
                                   cw, n[3], wr, wc, wo, batch, seq, tm_mix)
        qs, ks, kzs, vs, gs, bgs, a_s, gatess = _inproj(hs, n[2], wi, cos_s, sin_s, zeta_s)
        buf = state_conv[l]
        e2 = jnp.pad(buf, ((0, 0), (0, dec_seq - (CONV_W - 1)), (0, 0))).reshape(n_s, D_CONV)
        e1 = jnp.pad(buf[:, 1:], ((0, 0), (0, dec_seq - 1), (0, 0))).reshape(n_s, D_CONV)
        hs, ss_all = _mixer_sample(gch_s, hs, qs, ks, kzs, vs, gs, bgs, a_s, e1, e2, gatess,
                                   state_ret, l, ss_all, dmat_s, xi_s, cw, n[3], wr, wc, wo, bb)

        hp = _ffn(hp, n[4], n[5], wu2, wd2, tm_p)
        hs = _ffn(hs, n[4], n[5], wu2, wd2, n_s)

        sp_list.append(sp)
        bp_list.append(bp)
        bs_list.append(a_s.reshape(dec_batch, dec_seq, D_CONV)[:, dec_seq - (CONV_W - 1):])

    return (hp.reshape(batch, seq, D_MODEL), hs.reshape(dec_batch, dec_seq, D_MODEL),
            jnp.stack(sp_list), jnp.stack(bp_list), ss_all, jnp.stack(bs_list))
```

```python
import functools
import math

import jax
import jax.numpy as jnp
from jax import lax
from jax.experimental import pallas as pl
from jax.experimental.pallas import tpu as pltpu

D_MODEL = 1024
N_HEADS = 4
HD_QK = 128
HD_V = 256
D_QK = N_HEADS * HD_QK
D_V = N_HEADS * HD_V
D_CONV = 1024
CONV_W = 3
D_FF = 2816
CHUNK = 128
ROPE_BASE = 10000.0
NORM_EPS = 1e-6
PAST_LEN = 16384
K_SCALE = HD_QK ** -0.5

_OFF_Q, _OFF_K, _OFF_V, _OFF_G, _OFF_BG, _OFF_CG, _OFF_XC, _OFF_GATES = (
    0, 512, 1024, 2048, 3072, 4096, 5120, 6144)
N_IN = 8192

VMEM_LIMIT_BYTES = 56 * 1024 * 1024
FF_COLS = 256
SEG_COLS = 512
SAMPLE_PAIR_ROWS = 8
HALO_BASE = 8
BF16_SUBLANES = 16

F32 = jnp.float32
BF16 = jnp.bfloat16


def _rms(x, g):
    ms = jnp.mean(x * x, axis=-1, keepdims=True)
    return x * lax.rsqrt(ms + NORM_EPS) * g


def _dot(a, b):
    return jnp.dot(a, b, preferred_element_type=F32)


def _dot_nt(a, b):
    return lax.dot_general(a, b, (((1,), (1,)), ((), ())), preferred_element_type=F32)


def _dot_tn(a, b):
    return lax.dot_general(a, b, (((0,), (0,)), ((), ())), preferred_element_type=F32)


def _const_spec(shape):
    return pl.BlockSpec(shape, lambda *_: (0,) * len(shape), pipeline_mode=pl.Buffered(1))


def _params(n_axes):
    return pltpu.CompilerParams(
        dimension_semantics=("arbitrary",) * n_axes, vmem_limit_bytes=VMEM_LIMIT_BYTES)


def _ffn_kernel(x_ref, nin_ref, nout_ref, wup_ref, wdn_ref, *rest, n_cast):
    cast_in, o_ref, cast_out, hid_ref = (
        rest[:n_cast], rest[n_cast], rest[n_cast + 1:2 * n_cast + 1], rest[-1])
    x = x_ref[...]
    u = _rms(x, nin_ref[...]).astype(BF16)
    for c in range(D_FF // FF_COLS):
        lo = c * FF_COLS
        gate = _dot(u, wup_ref[:, lo:lo + FF_COLS])
        up = _dot(u, wup_ref[:, D_FF + lo:D_FF + lo + FF_COLS])
        hid_ref[:, lo:lo + FF_COLS] = (gate * jax.nn.sigmoid(gate) * up).astype(BF16)
    y = _dot(hid_ref[...], wdn_ref[...])
    o_ref[...] = x + 0.5 * _rms(y, nout_ref[...])
    for src, dst in zip(cast_in, cast_out):
        dst[...] = src[...].astype(BF16)


def _cast_rows_per_block(n_rows, steps):
    while n_rows % steps or (n_rows // steps) % BF16_SUBLANES:
        assert steps % 2 == 0, (n_rows, steps)
        steps //= 2
    return n_rows // steps


def _ffn(x, n_in, n_out, w_up, w_down, tm, cast=()):
    rows = x.shape[0]
    steps = rows // tm
    row_spec = pl.BlockSpec((tm, D_MODEL), lambda i: (i, 0))
    cast_in_specs, cast_out_specs, cast_shapes = [], [], []
    for w, layer in cast:
        _, w_rows, w_cols = w.shape
        block_rows = _cast_rows_per_block(w_rows, steps)
        every = steps // (w_rows // block_rows)
        cast_in_specs.append(pl.BlockSpec(
            (None, block_rows, w_cols), lambda i, layer=layer, every=every: (layer, i // every, 0)))
        cast_out_specs.append(pl.BlockSpec(
            (block_rows, w_cols), lambda i, every=every: (i // every, 0)))
        cast_shapes.append(jax.ShapeDtypeStruct((w_rows, w_cols), BF16))
    outs = pl.pallas_call(
        functools.partial(_ffn_kernel, n_cast=len(cast)),
        grid=(steps,),
        in_specs=[row_spec, _const_spec((1, D_MODEL)), _const_spec((1, D_MODEL)),
                  _const_spec((D_MODEL, 2 * D_FF)), _const_spec((D_FF, D_MODEL))] + cast_in_specs,
        out_specs=[row_spec] + cast_out_specs,
        out_shape=[jax.ShapeDtypeStruct((rows, D_MODEL), F32)] + cast_shapes,
        scratch_shapes=[pltpu.VMEM((tm, D_FF), BF16)],
        compiler_params=_params(1),
        name="ffn",
    )(x, n_in, n_out, w_up, w_down, *[w for w, _ in cast])
    return outs[0], outs[1:]


def _project_qk(seg, cos, sin, zeta_ref, q_ref, k_ref, kz_ref):
    def rotary(x):
        return x * cos + pltpu.roll(x, HD_QK // 2, 1) * sin

    q = seg(_OFF_Q)
    k = seg(_OFF_K)
    for h in range(N_HEADS):
        cols = slice(h * HD_QK, (h + 1) * HD_QK)
        q_ref[:, cols] = rotary(q[:, cols]).astype(q_ref.dtype)
        kh = rotary(k[:, cols]) * K_SCALE
        k_ref[:, cols] = kh.astype(k_ref.dtype)
        kz_ref[:, cols] = (kh * zeta_ref[:, cols]).astype(kz_ref.dtype)


def _head_out(o, g):
    on = o * lax.rsqrt(jnp.mean(o * o, axis=-1, keepdims=True) + NORM_EPS)
    return g * jax.nn.sigmoid(g) * on


def _merge_and_project(h, ret_in, conv_in, gate_r, gate_c, n3, wr_ref, wc_ref, wo_ref):
    merged = gate_r * _dot(ret_in, wr_ref[...]) + gate_c * _dot(conv_in, wc_ref[...])
    m = _dot(merged.astype(BF16), wo_ref[...])
    return h + _rms(m, n3)


def _mixer_prompt_kernel(gch_ref, h_ref, nin_ref, win_ref, cos_ref, sin_ref, zeta_ref,
                         dmat_ref, xi_ref, cw_ref, n3_ref, wr_ref, wc_ref, wo_ref,
                         hout_ref, s_ref, cs_ref,
                         q_sc, k_sc, kz_sc, v_sc, g_sc, halo_sc, conv_sc, gate_sc, ret_sc):
    tm = h_ref.shape[0]
    halo = CONV_W - 1
    base = HALO_BASE

    @pl.when(pl.program_id(1) == 0)
    def _():
        s_ref[...] = jnp.zeros_like(s_ref)
        halo_sc[0:base, :] = jnp.zeros((base, D_CONV), F32)

    x = h_ref[...]
    u = _rms(x, nin_ref[...]).astype(BF16)

    def seg(lo):
        return _dot(u, win_ref[:, lo:lo + SEG_COLS])

    _project_qk(seg, cos_ref[...], sin_ref[...], zeta_ref, q_sc, k_sc, kz_sc)
    for c in range(D_V // SEG_COLS):
        cols = slice(c * SEG_COLS, (c + 1) * SEG_COLS)
        v_sc[:, cols] = seg(_OFF_V + c * SEG_COLS).astype(BF16)
        g_sc[:, cols] = seg(_OFF_G + c * SEG_COLS)
        a = seg(_OFF_CG + c * SEG_COLS) * seg(_OFF_XC + c * SEG_COLS)
        halo_sc[base:base + tm, cols] = a
        z = (cw_ref[0:1, cols] * halo_sc[base - 2:base - 2 + tm, cols]
             + cw_ref[1:2, cols] * halo_sc[base - 1:base - 1 + tm, cols]
             + cw_ref[2:3, cols] * a)
        conv_sc[:, cols] = (seg(_OFF_BG + c * SEG_COLS) * z).astype(BF16)
        halo_sc[base - halo:base, cols] = a[tm - halo:tm, :]
        cs_ref[0, :, cols] = a[tm - halo:tm, :]
    for c in range(2 * D_MODEL // SEG_COLS):
        cols = slice(c * SEG_COLS, (c + 1) * SEG_COLS)
        gate_sc[:, cols] = jax.nn.sigmoid(seg(_OFF_GATES + c * SEG_COLS))

    for c in range(tm // CHUNK):
        rows = slice(c * CHUNK, (c + 1) * CHUNK)
        for hh in range(N_HEADS):
            qk_cols = slice(hh * HD_QK, (hh + 1) * HD_QK)
            v_cols = slice(hh * HD_V, (hh + 1) * HD_V)
            qc = q_sc[rows, qk_cols]
            vc = v_sc[rows, v_cols]
            scores = _dot_nt(qc, k_sc[rows, qk_cols]) * dmat_ref[hh]
            inner = _dot(scores.astype(BF16), vc)
            s = s_ref[0, hh]
            cross = _dot(qc, s.astype(BF16)) * xi_ref[hh]
            s_ref[0, hh] = gch_ref[hh] * s + _dot_tn(kz_sc[rows, qk_cols], vc)
            ret_sc[rows, v_cols] = _head_out(inner + cross, g_sc[rows, v_cols]).astype(BF16)

    hout_ref[...] = _merge_and_project(
        x, ret_sc[...], conv_sc[...], gate_sc[:, :D_MODEL], gate_sc[:, D_MODEL:],
        n3_ref[...], wr_ref, wc_ref, wo_ref)


def _mixer_prompt(gch, h, n2, w_in, cos2, sin2, zeta, dmat, xi, conv_w, n3, wr, wc, wo,
                  batch, seq, tm):
    nt = seq // tm
    row_spec = pl.BlockSpec((tm, D_MODEL), lambda b, t: (b * nt + t, 0))
    pos_spec = pl.BlockSpec((tm, HD_QK), lambda b, t: (t, 0))
    w_spec = _const_spec((D_MODEL, D_MODEL))
    return pl.pallas_call(
        _mixer_prompt_kernel,
        grid=(batch, nt),
        in_specs=[pl.BlockSpec(memory_space=pltpu.SMEM),
                  row_spec, _const_spec((1, D_MODEL)), _const_spec((D_MODEL, N_IN)),
                  pos_spec, pos_spec, _const_spec((tm, D_QK)),
                  _const_spec((N_HEADS, CHUNK, CHUNK)), _const_spec((N_HEADS, CHUNK, HD_V)),
                  _const_spec((CONV_W, D_CONV)), _const_spec((1, D_MODEL)),
                  w_spec, w_spec, w_spec],
        out_specs=[row_spec,
                   pl.BlockSpec((1, N_HEADS, HD_QK, HD_V), lambda b, t: (b, 0, 0, 0)),
                   pl.BlockSpec((1, CONV_W - 1, D_CONV), lambda b, t: (b, 0, 0))],
        out_shape=[jax.ShapeDtypeStruct((batch * seq, D_MODEL), F32),
                   jax.ShapeDtypeStruct((batch, N_HEADS, HD_QK, HD_V), F32),
                   jax.ShapeDtypeStruct((batch, CONV_W - 1, D_CONV), F32)],
        scratch_shapes=[pltpu.VMEM((tm, D_QK), BF16), pltpu.VMEM((tm, D_QK), BF16),
                        pltpu.VMEM((tm, D_QK), BF16), pltpu.VMEM((tm, D_V), BF16),
                        pltpu.VMEM((tm, D_V), F32), pltpu.VMEM((tm + HALO_BASE, D_CONV), F32),
                        pltpu.VMEM((tm, D_CONV), BF16), pltpu.VMEM((tm, 2 * D_MODEL), F32),
                        pltpu.VMEM((tm, D_V), BF16)],
        compiler_params=_params(2),
        name="mixer_prompt",
    )(gch, h, n2, w_in, cos2, sin2, zeta, dmat, xi, conv_w, n3, wr, wc, wo)


def _inproj_kernel(h_ref, n_ref, w_ref, cos_ref, sin_ref, zeta_ref,
                   q_ref, k_ref, kz_ref, v_ref, g_ref, bg_ref, a_ref, gates_ref):
    u = _rms(h_ref[...], n_ref[...]).astype(BF16)

    def seg(lo):
        return _dot(u, w_ref[:, lo:lo + SEG_COLS])

    _project_qk(seg, cos_ref[...], sin_ref[...], zeta_ref, q_ref, k_ref, kz_ref)
    for c in range(D_V // SEG_COLS):
        cols = slice(c * SEG_COLS, (c + 1) * SEG_COLS)
        v_ref[:, cols] = seg(_OFF_V + c * SEG_COLS)
        g_ref[:, cols] = seg(_OFF_G + c * SEG_COLS)
        bg_ref[:, cols] = seg(_OFF_BG + c * SEG_COLS)
        a_ref[:, cols] = seg(_OFF_CG + c * SEG_COLS) * seg(_OFF_XC + c * SEG_COLS)
    for c in range(2 * D_MODEL // SEG_COLS):
        cols = slice(c * SEG_COLS, (c + 1) * SEG_COLS)
        gates_ref[:, cols] = seg(_OFF_GATES + c * SEG_COLS)


def _inproj(h, n2, w_in, cos2, sin2, zeta):
    rows = h.shape[0]

    def full(shape):
        return pl.BlockSpec(shape, lambda i: (0, 0))

    args = (h, n2, w_in, cos2, sin2, zeta)
    out_cols = (D_QK, D_QK, D_QK, D_V, D_V, D_CONV, D_CONV, 2 * D_MODEL)
    return pl.pallas_call(
        _inproj_kernel,
        grid=(1,),
        in_specs=[full(x.shape) for x in args],
        out_specs=[full((rows, c)) for c in out_cols],
        out_shape=[jax.ShapeDtypeStruct((rows, c), F32) for c in out_cols],
        compiler_params=_params(1),
        name="inproj",
    )(*args)


def _mixer_sample_kernel(gch_ref, h_ref, q_ref, k_ref, kz_ref, v_ref, g_ref, bg_ref, a_ref,
                         e1_ref, e2_ref, gates_ref, s_in_ref, dmat_ref, xi_ref, cw_ref, n3_ref,
                         wr_ref, wc_ref, wo_ref, *rest, n_state_copies):
    hout_ref, s_out_ref, ret_ref = rest[-3:]
    rows = h_ref.shape[0]
    dec_seq = SAMPLE_PAIR_ROWS // 2
    n_pairs = rows // SAMPLE_PAIR_ROWS

    a = a_ref[...]
    tok = lax.broadcasted_iota(jnp.int32, (rows, D_CONV), 0) % dec_seq
    prev1 = jnp.where(tok >= 1, pltpu.roll(a, 1, 0), e1_ref[...])
    prev2 = jnp.where(tok >= 2, pltpu.roll(a, 2, 0), e2_ref[...])
    z = cw_ref[0:1, :] * prev2 + cw_ref[1:2, :] * prev1 + cw_ref[2:3, :] * a
    conv_in = (bg_ref[...] * z).astype(BF16)

    first_qk = lax.broadcasted_iota(jnp.int32, (SAMPLE_PAIR_ROWS, HD_QK), 0) < dec_seq
    first_v = lax.broadcasted_iota(jnp.int32, (SAMPLE_PAIR_ROWS, HD_V), 0) < dec_seq

    def store_state(b, hh, val):
        if n_state_copies == 1:
            s_out_ref[b, hh] = val
        else:
            for l in range(n_state_copies):
                s_out_ref[l, b, hh] = val

    def pair_body(p, carry):
        r0 = pl.multiple_of(p * SAMPLE_PAIR_ROWS, SAMPLE_PAIR_ROWS)
        prow = pl.ds(r0, SAMPLE_PAIR_ROWS)
        for hh in range(N_HEADS):
            qk_cols = slice(hh * HD_QK, (hh + 1) * HD_QK)
            v_cols = slice(hh * HD_V, (hh + 1) * HD_V)
            q8 = q_ref[prow, qk_cols].astype(BF16)
            k8 = k_ref[prow, qk_cols].astype(BF16)
            kz8 = kz_ref[prow, qk_cols]
            v8 = v_ref[prow, v_cols].astype(BF16)
            scores = _dot_nt(q8, k8) * dmat_ref[hh]
            inner = _dot(scores.astype(BF16), v8)
            s0 = s_in_ref[2 * p, hh]
            s1 = s_in_ref[2 * p + 1, hh]
            cross = jnp.where(first_v, _dot(q8, s0.astype(BF16)), _dot(q8, s1.astype(BF16)))
            cross = cross * xi_ref[hh]
            kz0 = jnp.where(first_qk, kz8, 0.0).astype(BF16)
            kz1 = jnp.where(first_qk, 0.0, kz8).astype(BF16)
            store_state(2 * p, hh, gch_ref[hh] * s0 + _dot_tn(kz0, v8))
            store_state(2 * p + 1, hh, gch_ref[hh] * s1 + _dot_tn(kz1, v8))
            ret_ref[prow, v_cols] = _head_out(inner + cross, g_ref[prow, v_cols])
        return carry

    lax.fori_loop(0, n_pairs, pair_body, 0)

    gates = gates_ref[...]
    hout_ref[...] = _merge_and_project(
        h_ref[...], ret_ref[...].astype(BF16), conv_in,
        jax.nn.sigmoid(gates[:, :D_MODEL]), jax.nn.sigmoid(gates[:, D_MODEL:]),
        n3_ref[...], wr_ref, wc_ref, wo_ref)


def _mixer_sample(gch, h, q, k, kz, v, g, bg, a, e1, e2, gates, state_ret, layer, states_so_far,
                  dmat, xi, conv_w, n3, wr, wc, wo, bb):
    depth, dec_batch = state_ret.shape[:2]
    rows = bb * (SAMPLE_PAIR_ROWS // 2)

    def row_spec(cols):
        return pl.BlockSpec((rows, cols), lambda i: (i, 0))

    w_spec = _const_spec((D_MODEL, D_MODEL))
    state_block = (bb, N_HEADS, HD_QK, HD_V)
    in_specs = [pl.BlockSpec(memory_space=pltpu.SMEM),
                row_spec(D_MODEL), row_spec(D_QK), row_spec(D_QK), row_spec(D_QK),
                row_spec(D_V), row_spec(D_V), row_spec(D_CONV), row_spec(D_CONV),
                row_spec(D_CONV), row_spec(D_CONV), row_spec(2 * D_MODEL),
                pl.BlockSpec((None,) + state_block, lambda i: (layer, i, 0, 0, 0)),
                _const_spec((N_HEADS, SAMPLE_PAIR_ROWS, SAMPLE_PAIR_ROWS)),
                _const_spec((N_HEADS, SAMPLE_PAIR_ROWS, HD_V)),
                _const_spec((CONV_W, D_CONV)), _const_spec((1, D_MODEL)),
                w_spec, w_spec, w_spec]
    args = [gch, h, q, k, kz, v, g, bg, a, e1, e2, gates, state_ret, dmat, xi, conv_w, n3,
            wr, wc, wo]
    if states_so_far is None:
        n_copies = depth
        state_spec = pl.BlockSpec((depth,) + state_block, lambda i: (0, i, 0, 0, 0))
        aliases = {}
    else:
        n_copies = 1
        state_spec = pl.BlockSpec((None,) + state_block, lambda i: (layer, i, 0, 0, 0))
        in_specs.append(pl.BlockSpec(memory_space=pl.ANY))
        args.append(states_so_far)
        aliases = {len(args) - 1: 1}
    return pl.pallas_call(
        functools.partial(_mixer_sample_kernel, n_state_copies=n_copies),
        grid=(dec_batch // bb,),
        in_specs=in_specs,
        out_specs=[row_spec(D_MODEL), state_spec],
        out_shape=[jax.ShapeDtypeStruct(h.shape, F32),
                   jax.ShapeDtypeStruct(state_ret.shape, F32)],
        scratch_shapes=[pltpu.VMEM((rows, D_V), F32)],
        input_output_aliases=aliases,
        compiler_params=_params(1),
        name="mixer_sample",
    )(*args)


def _log_gammas():
    gam = 1.0 - jnp.exp(jnp.linspace(math.log(1.0 / 32), math.log(1.0 / 512), N_HEADS, dtype=F32))
    return jnp.log(gam)


def _rotary_tables(pos):
    half = HD_QK // 2
    inv_freq = ROPE_BASE ** (-(jnp.arange(half, dtype=F32) * 2.0 / HD_QK))
    ang = pos.astype(F32)[:, None] * inv_freq[None, :]
    cos, sin = jnp.cos(ang), jnp.sin(ang)
    return jnp.concatenate([cos, cos], axis=-1), jnp.concatenate([-sin, sin], axis=-1)


def _decay_tables(c, lg):
    idx = jnp.arange(c, dtype=F32)
    diff = idx[:, None] - idx[None, :]
    dmat = jnp.where(diff[None] >= 0, jnp.exp(jnp.maximum(diff, 0.0)[None] * lg[:, None, None]), 0.0)
    xi = jnp.exp((idx[:, None] + 1.0) * lg[None, :])
    zeta = jnp.exp((c - 1.0 - idx)[:, None] * lg[None, :])
    return dmat, xi.T, zeta, jnp.exp(c * lg)


def _lanes(t, width):
    return jnp.repeat(t, width, axis=1)


def kernel(x_prompt, x_sample, state_ret, state_conv, norms, w_ffn1_up, w_ffn1_down, w_in, conv_w,
           w_ret_out, w_conv_out, w_o, w_ffn2_up, w_ffn2_down):
    batch, seq, _ = x_prompt.shape
    dec_batch, dec_seq, _ = x_sample.shape
    depth = norms.shape[0]
    assert seq % CHUNK == 0 and 2 * dec_seq == SAMPLE_PAIR_ROWS and CHUNK % dec_seq == 0
    tm_p = 512
    tm_mix = 512
    n_s = dec_batch * dec_seq
    bb = 8

    lg = _log_gammas()
    cos_p, sin_p = _rotary_tables(jnp.arange(seq, dtype=jnp.int32))
    dmat_p, xi_p, zeta_p, gch_p = _decay_tables(CHUNK, lg)
    zeta_p = jnp.tile(_lanes(zeta_p, HD_QK), (tm_mix // CHUNK, 1))
    xi_p = jnp.broadcast_to(xi_p[:, :, None], (N_HEADS, CHUNK, HD_V))
    cos_s, sin_s = _rotary_tables(PAST_LEN + jnp.arange(dec_seq, dtype=jnp.int32))
    cos_s = jnp.tile(cos_s, (dec_batch, 1))
    sin_s = jnp.tile(sin_s, (dec_batch, 1))
    dmat_s, xi_s, zeta_s, gch_s = _decay_tables(dec_seq, lg)
    zeta_s = jnp.tile(_lanes(zeta_s, HD_QK), (dec_batch, 1))
    eye2 = jnp.eye(2, dtype=F32)
    dmat_s = jnp.einsum("pq,hij->hpiqj", eye2, dmat_s).reshape(
        N_HEADS, SAMPLE_PAIR_ROWS, SAMPLE_PAIR_ROWS)
    xi_s = jnp.broadcast_to(jnp.tile(xi_s, (1, 2))[:, :, None], (N_HEADS, SAMPLE_PAIR_ROWS, HD_V))

    hp = x_prompt.reshape(batch * seq, D_MODEL)
    hs = x_sample.reshape(n_s, D_MODEL)
    sp_list, bp_list, bs_list = [], [], []
    ss_all = None
    wu1, wd1 = w_ffn1_up[0].astype(BF16), w_ffn1_down[0].astype(BF16)
    for l in range(depth):
        n = [norms[l, i][None, :] for i in range(norms.shape[1])]
        cw = conv_w[l]

        hp, (wi, wr, wc, wo, wu2, wd2) = _ffn(
            hp, n[0], n[1], wu1, wd1, tm_p,
            cast=[(w, l) for w in (w_in, w_ret_out, w_conv_out, w_o, w_ffn2_up, w_ffn2_down)])
        hs, _ = _ffn(hs, n[0], n[1], wu1, wd1, n_s)

        hp, sp, bp = _mixer_prompt(gch_p, hp, n[2], wi, cos_p, sin_p, zeta_p, dmat_p, xi_p,
                                   cw, n[3], wr, wc, wo, batch, seq, tm_mix)
        qs, ks, kzs, vs, gs, bgs, a_s, gatess = _inproj(hs, n[2], wi, cos_s, sin_s, zeta_s)
        buf = state_conv[l]
        e2 = jnp.pad(buf, ((0, 0), (0, dec_seq - (CONV_W - 1)), (0, 0))).reshape(n_s, D_CONV)
        e1 = jnp.pad(buf[:, 1:], ((0, 0), (0, dec_seq - 1), (0, 0))).reshape(n_s, D_CONV)
        hs, ss_all = _mixer_sample(gch_s, hs, qs, ks, kzs, vs, gs, bgs, a_s, e1, e2, gatess,
                                   state_ret, l, ss_all, dmat_s, xi_s, cw, n[3], wr, wc, wo, bb)

        next_ffn1 = [(w, l + 1) for w in (w_ffn1_up, w_ffn1_down)] if l + 1 < depth else []
        hp, next_w = _ffn(hp, n[4], n[5], wu2, wd2, tm_p, cast=next_ffn1)
        hs, _ = _ffn(hs, n[4], n[5], wu2, wd2, n_s)
        if next_w:
            wu1, wd1 = next_w

        sp_list.append(sp)
        bp_list.append(bp)
        bs_list.append(a_s.reshape(dec_batch, dec_seq, D_CONV)[:, dec_seq - (CONV_W - 1):])

    return (hp.reshape(batch, seq, D_MODEL), hs.reshape(dec_batch, dec_seq, D_MODEL),
            jnp.stack(sp_list), jnp.stack(bp_list), ss_all, jnp.stack(bs_list))
```

```python
import functools
import math

import jax
import jax.numpy as jnp
from jax import lax
from jax.experimental import pallas as pl
from jax.experimental.pallas import tpu as pltpu

D_MODEL = 1024
N_HEADS = 4
HD_QK = 128
HD_V = 256
D_QK = N_HEADS * HD_QK
D_V = N_HEADS * HD_V
D_CONV = 1024
CONV_W = 3
D_FF = 2816
CHUNK = 128
ROPE_BASE = 10000.0
NORM_EPS = 1e-6
PAST_LEN = 16384
K_SCALE = HD_QK ** -0.5

_OFF_Q, _OFF_K, _OFF_V, _OFF_G, _OFF_BG, _OFF_CG, _OFF_XC, _OFF_GATES = (
    0, 512, 1024, 2048, 3072, 4096, 5120, 6144)
N_IN = 8192

VMEM_LIMIT_BYTES = 56 * 1024 * 1024
FF_COLS = 256
SEG_COLS = 512
SAMPLE_PAIR_ROWS = 8
HALO_BASE = 8
BF16_SUBLANES = 16
FFN_SUBTILES = 2
MIXER_SUBTILES = 2

F32 = jnp.float32
BF16 = jnp.bfloat16


def _rms(x, g):
    ms = jnp.mean(x * x, axis=-1, keepdims=True)
    return x * lax.rsqrt(ms + NORM_EPS) * g


def _dot(a, b):
    return jnp.dot(a, b, preferred_element_type=F32)


def _dot_nt(a, b):
    return lax.dot_general(a, b, (((1,), (1,)), ((), ())), preferred_element_type=F32)


def _dot_tn(a, b):
    return lax.dot_general(a, b, (((0,), (0,)), ((), ())), preferred_element_type=F32)


def _const_spec(shape):
    return pl.BlockSpec(shape, lambda *_: (0,) * len(shape), pipeline_mode=pl.Buffered(1))


def _params(n_axes):
    return pltpu.CompilerParams(
        dimension_semantics=("arbitrary",) * n_axes, vmem_limit_bytes=VMEM_LIMIT_BYTES)


def _ffn_kernel(x_ref, nin_ref, nout_ref, wup_ref, wdn_ref, *rest, n_cast):
    cast_in, o_ref, cast_out, hid_ref = (
        rest[:n_cast], rest[n_cast], rest[n_cast + 1:2 * n_cast + 1], rest[-1])
    sub = x_ref.shape[0] // FFN_SUBTILES
    for s in range(FFN_SUBTILES):
        rows = slice(s * sub, (s + 1) * sub)
        x = x_ref[rows, :]
        u = _rms(x, nin_ref[...]).astype(BF16)
        for c in range(D_FF // FF_COLS):
            lo = c * FF_COLS
            gate = _dot(u, wup_ref[:, lo:lo + FF_COLS])
            up = _dot(u, wup_ref[:, D_FF + lo:D_FF + lo + FF_COLS])
            hid_ref[rows, lo:lo + FF_COLS] = (gate * jax.nn.sigmoid(gate) * up).astype(BF16)
        y = _dot(hid_ref[rows, :], wdn_ref[...])
        o_ref[rows, :] = x + 0.5 * _rms(y, nout_ref[...])
    for src, dst in zip(cast_in, cast_out):
        dst[...] = src[...].astype(BF16)


def _cast_rows_per_block(n_rows, steps):
    while n_rows % steps or (n_rows // steps) % BF16_SUBLANES:
        assert steps % 2 == 0, (n_rows, steps)
        steps //= 2
    return n_rows // steps


def _ffn(x, n_in, n_out, w_up, w_down, tm, cast=()):
    rows = x.shape[0]
    steps = rows // tm
    row_spec = pl.BlockSpec((tm, D_MODEL), lambda i: (i, 0))
    cast_in_specs, cast_out_specs, cast_shapes = [], [], []
    for w, layer in cast:
        _, w_rows, w_cols = w.shape
        block_rows = _cast_rows_per_block(w_rows, steps)
        every = steps // (w_rows // block_rows)
        cast_in_specs.append(pl.BlockSpec(
            (None, block_rows, w_cols), lambda i, layer=layer, every=every: (layer, i // every, 0)))
        cast_out_specs.append(pl.BlockSpec(
            (block_rows, w_cols), lambda i, every=every: (i // every, 0)))
        cast_shapes.append(jax.ShapeDtypeStruct((w_rows, w_cols), BF16))
    outs = pl.pallas_call(
        functools.partial(_ffn_kernel, n_cast=len(cast)),
        grid=(steps,),
        in_specs=[row_spec, _const_spec((1, D_MODEL)), _const_spec((1, D_MODEL)),
                  _const_spec((D_MODEL, 2 * D_FF)), _const_spec((D_FF, D_MODEL))] + cast_in_specs,
        out_specs=[row_spec] + cast_out_specs,
        out_shape=[jax.ShapeDtypeStruct((rows, D_MODEL), F32)] + cast_shapes,
        scratch_shapes=[pltpu.VMEM((tm, D_FF), BF16)],
        compiler_params=_params(1),
        name="ffn",
    )(x, n_in, n_out, w_up, w_down, *[w for w, _ in cast])
    return outs[0], outs[1:]


def _project_qk(seg, cos, sin, zeta_ref, q_ref, k_ref, kz_ref, rows=slice(None)):
    def rotary(x):
        return x * cos + pltpu.roll(x, HD_QK // 2, 1) * sin

    q = seg(_OFF_Q)
    k = seg(_OFF_K)
    for h in range(N_HEADS):
        cols = slice(h * HD_QK, (h + 1) * HD_QK)
        q_ref[rows, cols] = rotary(q[:, cols]).astype(q_ref.dtype)
        kh = rotary(k[:, cols]) * K_SCALE
        k_ref[rows, cols] = kh.astype(k_ref.dtype)
        kz_ref[rows, cols] = (kh * zeta_ref[rows, cols]).astype(kz_ref.dtype)


def _head_out(o, g):
    on = o * lax.rsqrt(jnp.mean(o * o, axis=-1, keepdims=True) + NORM_EPS)
    return g * jax.nn.sigmoid(g) * on


def _merge_and_project(h, ret_in, conv_in, gate_r, gate_c, n3, wr_ref, wc_ref, wo_ref):
    merged = gate_r * _dot(ret_in, wr_ref[...]) + gate_c * _dot(conv_in, wc_ref[...])
    m = _dot(merged.astype(BF16), wo_ref[...])
    return h + _rms(m, n3)


def _mixer_prompt_kernel(gch_ref, h_ref, nin_ref, win_ref, cos_ref, sin_ref, zeta_ref,
                         dmat_ref, xi_ref, cw_ref, n3_ref, wr_ref, wc_ref, wo_ref,
                         hout_ref, s_ref, cs_ref,
                         q_sc, k_sc, kz_sc, v_sc, g_sc, halo_sc, conv_sc, gate_sc, ret_sc):
    tm = h_ref.shape[0]
    halo = CONV_W - 1
    base = HALO_BASE

    @pl.when(pl.program_id(1) == 0)
    def _():
        s_ref[...] = jnp.zeros_like(s_ref)
        halo_sc[0:base, :] = jnp.zeros((base, D_CONV), F32)

    sub = tm // MIXER_SUBTILES
    for st in range(MIXER_SUBTILES):
        r0 = st * sub
        srows = slice(r0, r0 + sub)
        x = h_ref[srows, :]
        u = _rms(x, nin_ref[...]).astype(BF16)

        def seg(lo, u=u):
            return _dot(u, win_ref[:, lo:lo + SEG_COLS])

        _project_qk(seg, cos_ref[srows, :], sin_ref[srows, :], zeta_ref, q_sc, k_sc, kz_sc, srows)
        for c in range(D_V // SEG_COLS):
            cols = slice(c * SEG_COLS, (c + 1) * SEG_COLS)
            v_sc[srows, cols] = seg(_OFF_V + c * SEG_COLS).astype(BF16)
            g_sc[srows, cols] = seg(_OFF_G + c * SEG_COLS)
            a = seg(_OFF_CG + c * SEG_COLS) * seg(_OFF_XC + c * SEG_COLS)
            halo_sc[base + r0:base + r0 + sub, cols] = a
            z = (cw_ref[0:1, cols] * halo_sc[base + r0 - 2:base + r0 - 2 + sub, cols]
                 + cw_ref[1:2, cols] * halo_sc[base + r0 - 1:base + r0 - 1 + sub, cols]
                 + cw_ref[2:3, cols] * a)
            conv_sc[srows, cols] = (seg(_OFF_BG + c * SEG_COLS) * z).astype(BF16)
            if st == MIXER_SUBTILES - 1:
                halo_sc[base - halo:base, cols] = a[sub - halo:sub, :]
                cs_ref[0, :, cols] = a[sub - halo:sub, :]
        for c in range(2 * D_MODEL // SEG_COLS):
            cols = slice(c * SEG_COLS, (c + 1) * SEG_COLS)
            gate_sc[srows, cols] = jax.nn.sigmoid(seg(_OFF_GATES + c * SEG_COLS))

        qk_cols = [slice(hh * HD_QK, (hh + 1) * HD_QK) for hh in range(N_HEADS)]
        v_cols = [slice(hh * HD_V, (hh + 1) * HD_V) for hh in range(N_HEADS)]
        for c in range(sub // CHUNK):
            rows = slice(r0 + c * CHUNK, r0 + (c + 1) * CHUNK)
            scores = [_dot_nt(q_sc[rows, qk_cols[hh]], k_sc[rows, qk_cols[hh]]) * dmat_ref[hh]
                      for hh in range(N_HEADS)]
            update = [_dot_tn(kz_sc[rows, qk_cols[hh]], v_sc[rows, v_cols[hh]])
                      for hh in range(N_HEADS)]
            cross = []
            for hh in range(N_HEADS):
                s = s_ref[0, hh]
                cross.append(_dot(q_sc[rows, qk_cols[hh]], s.astype(BF16)) * xi_ref[hh])
                s_ref[0, hh] = gch_ref[hh] * s + update[hh]
            for hh in range(N_HEADS):
                inner = _dot(scores[hh].astype(BF16), v_sc[rows, v_cols[hh]])
                ret_sc[rows, v_cols[hh]] = _head_out(
                    inner + cross[hh], g_sc[rows, v_cols[hh]]).astype(BF16)

        hout_ref[srows, :] = _merge_and_project(
            x, ret_sc[srows, :], conv_sc[srows, :], gate_sc[srows, :D_MODEL],
            gate_sc[srows, D_MODEL:], n3_ref[...], wr_ref, wc_ref, wo_ref)


def _mixer_prompt(gch, h, n2, w_in, cos2, sin2, zeta, dmat, xi, conv_w, n3, wr, wc, wo,
                  batch, seq, tm):
    nt = seq // tm
    row_spec = pl.BlockSpec((tm, D_MODEL), lambda b, t: (b * nt + t, 0))
    pos_spec = pl.BlockSpec((tm, HD_QK), lambda b, t: (t, 0))
    w_spec = _const_spec((D_MODEL, D_MODEL))
    return pl.pallas_call(
        _mixer_prompt_kernel,
        grid=(batch, nt),
        in_specs=[pl.BlockSpec(memory_space=pltpu.SMEM),
                  row_spec, _const_spec((1, D_MODEL)), _const_spec((D_MODEL, N_IN)),
                  pos_spec, pos_spec, _const_spec((tm, D_QK)),
                  _const_spec((N_HEADS, CHUNK, CHUNK)), _const_spec((N_HEADS, CHUNK, HD_V)),
                  _const_spec((CONV_W, D_CONV)), _const_spec((1, D_MODEL)),
                  w_spec, w_spec, w_spec],
        out_specs=[row_spec,
                   pl.BlockSpec((1, N_HEADS, HD_QK, HD_V), lambda b, t: (b, 0, 0, 0)),
                   pl.BlockSpec((1, CONV_W - 1, D_CONV), lambda b, t: (b, 0, 0))],
        out_shape=[jax.ShapeDtypeStruct((batch * seq, D_MODEL), F32),
                   jax.ShapeDtypeStruct((batch, N_HEADS, HD_QK, HD_V), F32),
                   jax.ShapeDtypeStruct((batch, CONV_W - 1, D_CONV), F32)],
        scratch_shapes=[pltpu.VMEM((tm, D_QK), BF16), pltpu.VMEM((tm, D_QK), BF16),
                        pltpu.VMEM((tm, D_QK), BF16), pltpu.VMEM((tm, D_V), BF16),
                        pltpu.VMEM((tm, D_V), F32), pltpu.VMEM((tm + HALO_BASE, D_CONV), F32),
                        pltpu.VMEM((tm, D_CONV), BF16), pltpu.VMEM((tm, 2 * D_MODEL), F32),
                        pltpu.VMEM((tm, D_V), BF16)],
        compiler_params=_params(2),
        name="mixer_prompt",
    )(gch, h, n2, w_in, cos2, sin2, zeta, dmat, xi, conv_w, n3, wr, wc, wo)


def _inproj_kernel(h_ref, n_ref, w_ref, cos_ref, sin_ref, zeta_ref,
                   q_ref, k_ref, kz_ref, v_ref, g_ref, bg_ref, a_ref, gates_ref):
    u = _rms(h_ref[...], n_ref[...]).astype(BF16)

    def seg(lo):
        return _dot(u, w_ref[:, lo:lo + SEG_COLS])

    _project_qk(seg, cos_ref[...], sin_ref[...], zeta_ref, q_ref, k_ref, kz_ref)
    for c in range(D_V // SEG_COLS):
        cols = slice(c * SEG_COLS, (c + 1) * SEG_COLS)
        v_ref[:, cols] = seg(_OFF_V + c * SEG_COLS)
        g_ref[:, cols] = seg(_OFF_G + c * SEG_COLS)
        bg_ref[:, cols] = seg(_OFF_BG + c * SEG_COLS)
        a_ref[:, cols] = seg(_OFF_CG + c * SEG_COLS) * seg(_OFF_XC + c * SEG_COLS)
    for c in range(2 * D_MODEL // SEG_COLS):
        cols = slice(c * SEG_COLS, (c + 1) * SEG_COLS)
        gates_ref[:, cols] = seg(_OFF_GATES + c * SEG_COLS)


def _inproj(h, n2, w_in, cos2, sin2, zeta):
    rows = h.shape[0]

    def full(shape):
        return pl.BlockSpec(shape, lambda i: (0, 0))

    args = (h, n2, w_in, cos2, sin2, zeta)
    out_cols = (D_QK, D_QK, D_QK, D_V, D_V, D_CONV, D_CONV, 2 * D_MODEL)
    return pl.pallas_call(
        _inproj_kernel,
        grid=(1,),
        in_specs=[full(x.shape) for x in args],
        out_specs=[full((rows, c)) for c in out_cols],
        out_shape=[jax.ShapeDtypeStruct((rows, c), F32) for c in out_cols],
        compiler_params=_params(1),
        name="inproj",
    )(*args)


def _mixer_sample_kernel(gch_ref, h_ref, q_ref, k_ref, kz_ref, v_ref, g_ref, bg_ref, a_ref,
                         e1_ref, e2_ref, gates_ref, s_in_ref, dmat_ref, xi_ref, cw_ref, n3_ref,
                         wr_ref, wc_ref, wo_ref, *rest, n_state_copies):
    hout_ref, s_out_ref, ret_ref = rest[-3:]
    rows = h_ref.shape[0]
    dec_seq = SAMPLE_PAIR_ROWS // 2
    n_pairs = rows // SAMPLE_PAIR_ROWS

    a = a_ref[...]
    tok = lax.broadcasted_iota(jnp.int32, (rows, D_CONV), 0) % dec_seq
    prev1 = jnp.where(tok >= 1, pltpu.roll(a, 1, 0), e1_ref[...])
    prev2 = jnp.where(tok >= 2, pltpu.roll(a, 2, 0), e2_ref[...])
    z = cw_ref[0:1, :] * prev2 + cw_ref[1:2, :] * prev1 + cw_ref[2:3, :] * a
    conv_in = (bg_ref[...] * z).astype(BF16)

    first_qk = lax.broadcasted_iota(jnp.int32, (SAMPLE_PAIR_ROWS, HD_QK), 0) < dec_seq
    first_v = lax.broadcasted_iota(jnp.int32, (SAMPLE_PAIR_ROWS, HD_V), 0) < dec_seq

    def store_state(b, hh, val):
        if n_state_copies == 1:
            s_out_ref[b, hh] = val
        else:
            for l in range(n_state_copies):
                s_out_ref[l, b, hh] = val

    def pair_body(p, carry):
        r0 = pl.multiple_of(p * SAMPLE_PAIR_ROWS, SAMPLE_PAIR_ROWS)
        prow = pl.ds(r0, SAMPLE_PAIR_ROWS)
        for hh in range(N_HEADS):
            qk_cols = slice(hh * HD_QK, (hh + 1) * HD_QK)
            v_cols = slice(hh * HD_V, (hh + 1) * HD_V)
            q8 = q_ref[prow, qk_cols].astype(BF16)
            k8 = k_ref[prow, qk_cols].astype(BF16)
            kz8 = kz_ref[prow, qk_cols]
            v8 = v_ref[prow, v_cols].astype(BF16)
            scores = _dot_nt(q8, k8) * dmat_ref[hh]
            inner = _dot(scores.astype(BF16), v8)
            s0 = s_in_ref[2 * p, hh]
            s1 = s_in_ref[2 * p + 1, hh]
            cross = jnp.where(first_v, _dot(q8, s0.astype(BF16)), _dot(q8, s1.astype(BF16)))
            cross = cross * xi_ref[hh]
            kz0 = jnp.where(first_qk, kz8, 0.0).astype(BF16)
            kz1 = jnp.where(first_qk, 0.0, kz8).astype(BF16)
            store_state(2 * p, hh, gch_ref[hh] * s0 + _dot_tn(kz0, v8))
            store_state(2 * p + 1, hh, gch_ref[hh] * s1 + _dot_tn(kz1, v8))
            ret_ref[prow, v_cols] = _head_out(inner + cross, g_ref[prow, v_cols])
        return carry

    lax.fori_loop(0, n_pairs, pair_body, 0, unroll=True)

    gates = gates_ref[...]
    hout_ref[...] = _merge_and_project(
        h_ref[...], ret_ref[...].astype(BF16), conv_in,
        jax.nn.sigmoid(gates[:, :D_MODEL]), jax.nn.sigmoid(gates[:, D_MODEL:]),
        n3_ref[...], wr_ref, wc_ref, wo_ref)


def _mixer_sample(gch, h, q, k, kz, v, g, bg, a, e1, e2, gates, state_ret, layer, states_so_far,
                  dmat, xi, conv_w, n3, wr, wc, wo, bb):
    depth, dec_batch = state_ret.shape[:2]
    rows = bb * (SAMPLE_PAIR_ROWS // 2)

    def row_spec(cols):
        return pl.BlockSpec((rows, cols), lambda i: (i, 0))

    w_spec = _const_spec((D_MODEL, D_MODEL))
    state_block = (bb, N_HEADS, HD_QK, HD_V)
    in_specs = [pl.BlockSpec(memory_space=pltpu.SMEM),
                row_spec(D_MODEL), row_spec(D_QK), row_spec(D_QK), row_spec(D_QK),
                row_spec(D_V), row_spec(D_V), row_spec(D_CONV), row_spec(D_CONV),
                row_spec(D_CONV), row_spec(D_CONV), row_spec(2 * D_MODEL),
                pl.BlockSpec((None,) + state_block, lambda i: (layer, i, 0, 0, 0)),
                _const_spec((N_HEADS, SAMPLE_PAIR_ROWS, SAMPLE_PAIR_ROWS)),
                _const_spec((N_HEADS, SAMPLE_PAIR_ROWS, HD_V)),
                _const_spec((CONV_W, D_CONV)), _const_spec((1, D_MODEL)),
                w_spec, w_spec, w_spec]
    args = [gch, h, q, k, kz, v, g, bg, a, e1, e2, gates, state_ret, dmat, xi, conv_w, n3,
            wr, wc, wo]
    if states_so_far is None:
        n_copies = depth
        state_spec = pl.BlockSpec((depth,) + state_block, lambda i: (0, i, 0, 0, 0))
        aliases = {}
    else:
        n_copies = 1
        state_spec = pl.BlockSpec((None,) + state_block, lambda i: (layer, i, 0, 0, 0))
        in_specs.append(pl.BlockSpec(memory_space=pl.ANY))
        args.append(states_so_far)
        aliases = {len(args) - 1: 1}
    return pl.pallas_call(
        functools.partial(_mixer_sample_kernel, n_state_copies=n_copies),
        grid=(dec_batch // bb,),
        in_specs=in_specs,
        out_specs=[row_spec(D_MODEL), state_spec],
        out_shape=[jax.ShapeDtypeStruct(h.shape, F32),
                   jax.ShapeDtypeStruct(state_ret.shape, F32)],
        scratch_shapes=[pltpu.VMEM((rows, D_V), F32)],
        input_output_aliases=aliases,
        compiler_params=_params(1),
        name="mixer_sample",
    )(*args)


def _log_gammas():
    gam = 1.0 - jnp.exp(jnp.linspace(math.log(1.0 / 32), math.log(1.0 / 512), N_HEADS, dtype=F32))
    return jnp.log(gam)


def _rotary_tables(pos):
    half = HD_QK // 2
    inv_freq = ROPE_BASE ** (-(jnp.arange(half, dtype=F32) * 2.0 / HD_QK))
    ang = pos.astype(F32)[:, None] * inv_freq[None, :]
    cos, sin = jnp.cos(ang), jnp.sin(ang)
    return jnp.concatenate([cos, cos], axis=-1), jnp.concatenate([-sin, sin], axis=-1)


def _decay_tables(c, lg):
    idx = jnp.arange(c, dtype=F32)
    diff = idx[:, None] - idx[None, :]
    dmat = jnp.where(diff[None] >= 0, jnp.exp(jnp.maximum(diff, 0.0)[None] * lg[:, None, None]), 0.0)
    xi = jnp.exp((idx[:, None] + 1.0) * lg[None, :])
    zeta = jnp.exp((c - 1.0 - idx)[:, None] * lg[None, :])
    return dmat, xi.T, zeta, jnp.exp(c * lg)


def _lanes(t, width):
    return jnp.repeat(t, width, axis=1)


def kernel(x_prompt, x_sample, state_ret, state_conv, norms, w_ffn1_up, w_ffn1_down, w_in, conv_w,
           w_ret_out, w_conv_out, w_o, w_ffn2_up, w_ffn2_down):
    batch, seq, _ = x_prompt.shape
    dec_batch, dec_seq, _ = x_sample.shape
    depth = norms.shape[0]
    assert seq % CHUNK == 0 and 2 * dec_seq == SAMPLE_PAIR_ROWS and CHUNK % dec_seq == 0
    tm_p = 512
    tm_mix = 512
    n_s = dec_batch * dec_seq
    bb = 8

    lg = _log_gammas()
    cos_p, sin_p = _rotary_tables(jnp.arange(seq, dtype=jnp.int32))
    dmat_p, xi_p, zeta_p, gch_p = _decay_tables(CHUNK, lg)
    zeta_p = jnp.tile(_lanes(zeta_p, HD_QK), (tm_mix // CHUNK, 1))
    xi_p = jnp.broadcast_to(xi_p[:, :, None], (N_HEADS, CHUNK, HD_V))
    cos_s, sin_s = _rotary_tables(PAST_LEN + jnp.arange(dec_seq, dtype=jnp.int32))
    cos_s = jnp.tile(cos_s, (dec_batch, 1))
    sin_s = jnp.tile(sin_s, (dec_batch, 1))
    dmat_s, xi_s, zeta_s, gch_s = _decay_tables(dec_seq, lg)
    zeta_s = jnp.tile(_lanes(zeta_s, HD_QK), (dec_batch, 1))
    eye2 = jnp.eye(2, dtype=F32)
    dmat_s = jnp.einsum("pq,hij->hpiqj", eye2, dmat_s).reshape(
        N_HEADS, SAMPLE_PAIR_ROWS, SAMPLE_PAIR_ROWS)
    xi_s = jnp.broadcast_to(jnp.tile(xi_s, (1, 2))[:, :, None], (N_HEADS, SAMPLE_PAIR_ROWS, HD_V))

    hp = x_prompt.reshape(batch * seq, D_MODEL)
    hs = x_sample.reshape(n_s, D_MODEL)
    sp_list, bp_list, bs_list = [], [], []
    ss_all = None
    wu1, wd1 = w_ffn1_up[0].astype(BF16), w_ffn1_down[0].astype(BF16)
    for l in range(depth):
        n = [norms[l, i][None, :] for i in range(norms.shape[1])]
        cw = conv_w[l]

        hp, (wi, wr, wc, wo, wu2, wd2) = _ffn(
            hp, n[0], n[1], wu1, wd1, tm_p,
            cast=[(w, l) for w in (w_in, w_ret_out, w_conv_out, w_o, w_ffn2_up, w_ffn2_down)])
        hs, _ = _ffn(hs, n[0], n[1], wu1, wd1, n_s)

        hp, sp, bp = _mixer_prompt(gch_p, hp, n[2], wi, cos_p, sin_p, zeta_p, dmat_p, xi_p,
                                   cw, n[3], wr, wc, wo, batch, seq, tm_mix)
        qs, ks, kzs, vs, gs, bgs, a_s, gatess = _inproj(hs, n[2], wi, cos_s, sin_s, zeta_s)
        buf = state_conv[l]
        e2 = jnp.pad(buf, ((0, 0), (0, dec_seq - (CONV_W - 1)), (0, 0))).reshape(n_s, D_CONV)
        e1 = jnp.pad(buf[:, 1:], ((0, 0), (0, dec_seq - 1), (0, 0))).reshape(n_s, D_CONV)
        hs, ss_all = _mixer_sample(gch_s, hs, qs, ks, kzs, vs, gs, bgs, a_s, e1, e2, gatess,
                                   state_ret, l, ss_all, dmat_s, xi_s, cw, n[3], wr, wc, wo, bb)

        next_ffn1 = [(w, l + 1) for w in (w_ffn1_up, w_ffn1_down)] if l + 1 < depth else []
        hp, next_w = _ffn(hp, n[4], n[5], wu2, wd2, tm_p, cast=next_ffn1)
        hs, _ = _ffn(hs, n[4], n[5], wu2, wd2, n_s)
        if next_w:
            wu1, wd1 = next_w

        sp_list.append(sp)
        bp_list.append(bp)
        bs_list.append(a_s.reshape(dec_batch, dec_seq, D_CONV)[:, dec_seq - (CONV_W - 1):])

    return (hp.reshape(batch, seq, D_MODEL), hs.reshape(dec_batch, dec_seq, D_MODEL),
            jnp.stack(sp_list), jnp.stack(bp_list), ss_all, jnp.stack(bs_list))
```
